```python
import jax, jax.numpy as jnp
from jax import lax
import numpy as np

D_MODEL = 1024
BATCH = 8
SEQ = 2048
DEPTH = 4
DEC_BATCH = 32
DEC_SEQ = 8
PAST_LEN = 8192
PAGE_SIZE = 128

N_EVEN = (DEPTH + 1) // 2
N_ODD = DEPTH // 2
A_WIDTH = D_MODEL // 2
A_HEAD_DIM = 64
A_HEADS = A_WIDTH // A_HEAD_DIM
A_DECAY_RANK = 64
A_ICLR_RANK = 64
A_SHIFT_W = 3 * A_WIDTH + A_DECAY_RANK + A_ICLR_RANK
GN_EPS = 64e-5
B_WIDTH = D_MODEL // 2
CONV_W = 3
EVEN_IN_W = A_SHIFT_W + A_WIDTH + 4 * B_WIDTH
EVEN_MIX_W = A_WIDTH + B_WIDTH
C_GROUPS = ((128, 1), (512, 4), (2048, 16))
N_C_GROUPS = len(C_GROUPS)
C_HEADS = 8
C_HEAD_DIM = 64
C_WIDTH = C_HEADS * C_HEAD_DIM
C_QKV_W = 3 * N_C_GROUPS * C_WIDTH
ODD_IN_W = C_QKV_W + C_WIDTH
ATTN_BLOCK = 128
LN_EPS = 1e-5
DEEPNORM_ALPHA = (2 * DEPTH) ** 0.25
DEEPNORM_BETA = (8 * DEPTH) ** -0.25

kernel_name = 'hybrid_rwkv7_shortconv_dilated_swa_step'


def layer_norm(x, g, b):
    xf = x.astype(jnp.float32)
    mu = jnp.mean(xf, -1, keepdims=True)
    var = jnp.mean(jnp.square(xf - mu), -1, keepdims=True)
    return ((xf - mu) * lax.rsqrt(var + LN_EPS) * g + b).astype(x.dtype)


def wkv7_scan(r, decay, k, v, kk, a, s0):
    def step(S, inp):
        r_t, w_t, k_t, v_t, kk_t, a_t = inp
        sa = jnp.einsum('bhij,bhj->bhi', S, -kk_t)
        S = (S * w_t[:, :, None, :] + sa[..., None] * (kk_t * a_t)[:, :, None, :]
             + v_t[..., None] * k_t[:, :, None, :])
        return S, jnp.einsum('bhij,bhj->bhi', S, r_t)
    xs = tuple(jnp.moveaxis(t, 1, 0) for t in (r, decay, k, v, kk, a))
    s_final, ys = lax.scan(step, s0, xs)
    return jnp.moveaxis(ys, 0, 1), s_final


def rwkv7_mix(p, shift_prev, s0, mu, w0, w2, a0, a2, k_k, k_a, r_k, gn_g, gn_b):
    Bn, T, _ = p.shape
    prev = jnp.concatenate([shift_prev[:, None, :], p[:, :-1]], axis=1)
    ps = (p + (prev - p) * mu).astype(jnp.float32)
    r, k, v, wl, al = jnp.split(ps, [A_WIDTH, 2 * A_WIDTH, 3 * A_WIDTH, 3 * A_WIDTH + A_DECAY_RANK], axis=-1)
    w = -jax.nn.softplus(-(w0 + jnp.tanh(wl) @ w2)) - 0.5
    decay = jnp.exp(-jnp.exp(w))
    a = jax.nn.sigmoid(a0 + al @ a2)
    heads = lambda t: t.reshape(Bn, T, A_HEADS, A_HEAD_DIM)
    kk = heads(k * k_k)
    kk = kk * lax.rsqrt(jnp.maximum(jnp.sum(jnp.square(kk), -1, keepdims=True), 1e-24))
    k = k * (1.0 + (a - 1.0) * k_a)
    r, k, v, decay, a = heads(r), heads(k), heads(v), heads(decay), heads(a)
    y, s_new = wkv7_scan(r, decay, k, v, kk, a, s0.astype(jnp.float32))
    mean = jnp.mean(y, -1, keepdims=True)
    var = jnp.mean(jnp.square(y - mean), -1, keepdims=True)
    y = (y - mean) * lax.rsqrt(var + GN_EPS)
    y = y * gn_g.reshape(A_HEADS, A_HEAD_DIM) + gn_b.reshape(A_HEADS, A_HEAD_DIM)
    y = y + jnp.sum(r * k * r_k, -1, keepdims=True) * v
    return y.reshape(Bn, T, A_WIDTH), p[:, -1], s_new


def short_conv_mix(p, conv_prev, conv_w):
    b, c, h = jnp.split(p, 3, axis=-1)
    u = c * h
    T = u.shape[1]
    ucat = jnp.concatenate([conv_prev.astype(u.dtype), u], axis=1)
    y = sum(conv_w[j] * ucat[:, j:j + T] for j in range(CONV_W))
    return b * y, ucat[:, -(CONV_W - 1):]


def even_mix(x, shift_prev, s0, conv_prev, w_in, w_out, rw, conv_w):
    p = x @ w_in
    pA, zA, pB, zB = jnp.split(p, [A_SHIFT_W, A_SHIFT_W + A_WIDTH, A_SHIFT_W + A_WIDTH + 3 * B_WIDTH], axis=-1)
    yA, shift_new, s_new = rwkv7_mix(pA, shift_prev, s0, *rw)
    yB, conv_new = short_conv_mix(pB, conv_prev, conv_w)
    y = jnp.concatenate([yA.astype(x.dtype) * jax.nn.silu(zA), yB * jax.nn.silu(zB)], axis=-1) @ w_out
    return y, shift_new, s_new, conv_new


def dilated_window_attn_prompt(q, k, v, window, dilation):
    Bn, S, H, hd = q.shape
    nj = window // dilation
    L = S // dilation
    nb = -(-L // ATTN_BLOCK)
    Lp = nb * ATTN_BLOCK
    N = Bn * dilation
    def to_sub(t):
        t = t.reshape(Bn, L, dilation, H, hd).transpose(0, 2, 1, 3, 4).reshape(N, L, H, hd)
        return jnp.pad(t, ((0, 0), (0, Lp - L), (0, 0), (0, 0)))
    def band(t):
        tb = jnp.pad(t, ((0, 0), (ATTN_BLOCK, 0), (0, 0), (0, 0))).reshape(N, nb + 1, ATTN_BLOCK, H, hd)
        return jnp.concatenate([tb[:, :-1], tb[:, 1:]], axis=2)
    qb = to_sub(q).reshape(N, nb, ATTN_BLOCK, H, hd)
    kb, vb = band(to_sub(k)), band(to_sub(v))
    s = jnp.einsum('ncqhd,nckhd->nchqk', qb, kb, preferred_element_type=jnp.float32) * (hd ** -0.5)
    qi = jnp.arange(ATTN_BLOCK)[:, None]
    ki = jnp.arange(2 * ATTN_BLOCK)[None, :]
    dist = qi + ATTN_BLOCK - ki
    kpos = (jnp.arange(nb)[:, None, None] - 1) * ATTN_BLOCK + ki[None]
    mask = (dist >= 0) & (dist <= nj) & (kpos >= 0)
    s = jnp.where(mask[None, :, None], s, -jnp.inf)
    m = jnp.max(s, -1, keepdims=True)
    e = jnp.exp(s - m)
    den = jnp.sum(e, -1, keepdims=True)
    o = jnp.einsum('nchqk,nckhd->ncqhd', e / den, vb.astype(jnp.float32))
    lse = (m + jnp.log(den))[..., 0].transpose(0, 1, 3, 2)
    o = o.reshape(N, Lp, H, hd)[:, :L].reshape(Bn, dilation, L, H, hd).transpose(0, 2, 1, 3, 4).reshape(Bn, S, H, hd)
    lse = lse.reshape(N, Lp, H)[:, :L].reshape(Bn, dilation, L, H).transpose(0, 2, 1, 3).reshape(Bn, S, H)
    return o, lse


def dilated_window_attn_sample(q, k, v, kv_buf, window, dilation):
    Bn, T, H, hd = q.shape
    Wb = kv_buf.shape[1]
    nj = window // dilation
    kcat = jnp.concatenate([kv_buf[:, :, 0].astype(k.dtype), k], axis=1)
    vcat = jnp.concatenate([kv_buf[:, :, 1].astype(v.dtype), v], axis=1)
    idx = Wb + jnp.arange(T)[:, None] - dilation * jnp.arange(nj + 1)[None, :]
    valid = idx >= 0
    idxc = jnp.maximum(idx, 0)
    kg, vg = kcat[:, idxc], vcat[:, idxc]
    s = jnp.einsum('bthd,btjhd->bthj', q, kg, preferred_element_type=jnp.float32) * (hd ** -0.5)
    s = jnp.where(valid[None, :, None, :], s, -jnp.inf)
    m = jnp.max(s, -1, keepdims=True)
    e = jnp.exp(s - m)
    den = jnp.sum(e, -1, keepdims=True)
    o = jnp.einsum('bthj,btjhd->bthd', e / den, vg.astype(jnp.float32))
    lse = (m + jnp.log(den))[..., 0]
    keep = min(window, Wb + T)
    new_buf = jnp.stack([kcat, vcat], axis=2)[:, Wb + T - keep:]
    return o, lse, new_buf


def odd_mix(x, kv_bufs, w_in, w_out):
    Bn, T, _ = x.shape
    p = x @ w_in
    qkv = p[..., :C_QKV_W].reshape(Bn, T, 3, N_C_GROUPS, C_HEADS, C_HEAD_DIM)
    z = p[..., C_QKV_W:]
    outs, lses, new_bufs = [], [], []
    for g, (window, dilation) in enumerate(C_GROUPS):
        q, k, v = qkv[:, :, 0, g], qkv[:, :, 1, g], qkv[:, :, 2, g]
        if kv_bufs is None:
            o, lse = dilated_window_attn_prompt(q, k, v, window, dilation)
            keep = min(window, T)
            nbuf = jnp.stack([k, v], axis=2)[:, T - keep:]
        else:
            o, lse, nbuf = dilated_window_attn_sample(q, k, v, kv_bufs[g], window, dilation)
        outs.append(o)
        lses.append(lse)
        new_bufs.append(nbuf)
    alpha = jax.nn.softmax(jnp.stack(lses, axis=2), axis=2)
    o = jnp.sum(alpha[..., None] * jnp.stack(outs, axis=2), axis=2)
    y = o.reshape(Bn, T, C_WIDTH).astype(x.dtype) * jax.nn.silu(z)
    return y @ w_out, new_bufs


def setup_inputs(seed: int = 0) -> dict:
    key = jax.random.key(seed)
    ks = jax.random.split(key, 32)
    nrm = lambda k, shape, s: jax.random.normal(k, shape, jnp.float32) * s
    kv_shape = lambda w: (N_ODD, DEC_BATCH, min(w, PAST_LEN), 2, C_HEADS, C_HEAD_DIM)
    return {
        'x_prompt': nrm(ks[0], (BATCH, SEQ, D_MODEL), 1.0),
        'x_sample': nrm(ks[1], (DEC_BATCH, DEC_SEQ, D_MODEL), 1.0),
        'state_rwkv': nrm(ks[2], (N_EVEN, DEC_BATCH, A_HEADS, A_HEAD_DIM, A_HEAD_DIM), 0.1),
        'state_shift': nrm(ks[3], (N_EVEN, DEC_BATCH, A_SHIFT_W), 1.0),
        'state_conv': nrm(ks[4], (N_EVEN, DEC_BATCH, CONV_W - 1, B_WIDTH), 1.0),
        'cache_kv_w128': nrm(ks[5], kv_shape(128), 1.0),
        'cache_kv_w512': nrm(ks[6], kv_shape(512), 1.0),
        'cache_kv_w2048': nrm(ks[7], kv_shape(2048), 1.0),
        'even_w_in': nrm(ks[8], (N_EVEN, D_MODEL, EVEN_IN_W), D_MODEL ** -0.5),
        'even_w_out': nrm(ks[9], (N_EVEN, EVEN_MIX_W, D_MODEL), EVEN_MIX_W ** -0.5 * DEEPNORM_BETA),
        'rwkv_mu': jax.random.uniform(ks[10], (N_EVEN, A_SHIFT_W), jnp.float32),
        'rwkv_w0': -1.0 + nrm(ks[11], (N_EVEN, A_WIDTH), 0.5),
        'rwkv_w2': nrm(ks[12], (N_EVEN, A_DECAY_RANK, A_WIDTH), 0.1),
        'rwkv_a0': nrm(ks[13], (N_EVEN, A_WIDTH), 0.1),
        'rwkv_a2': nrm(ks[14], (N_EVEN, A_ICLR_RANK, A_WIDTH), 0.1),
        'rwkv_k_k': 0.85 + nrm(ks[15], (N_EVEN, A_WIDTH), 0.02),
        'rwkv_k_a': 1.0 + nrm(ks[16], (N_EVEN, A_WIDTH), 0.02),
        'rwkv_r_k': nrm(ks[17], (N_EVEN, A_HEADS, A_HEAD_DIM), 0.1),
        'rwkv_gn_g': 1.0 + nrm(ks[18], (N_EVEN, A_WIDTH), 0.02),
        'rwkv_gn_b': nrm(ks[19], (N_EVEN, A_WIDTH), 0.02),
        'conv_w': nrm(ks[20], (N_EVEN, CONV_W, B_WIDTH), CONV_W ** -0.5),
        'odd_w_in': nrm(ks[21], (N_ODD, D_MODEL, ODD_IN_W), D_MODEL ** -0.5),
        'odd_w_out': nrm(ks[22], (N_ODD, C_WIDTH, D_MODEL), C_WIDTH ** -0.5 * DEEPNORM_BETA),
        'ln_g': 1.0 + nrm(ks[23], (DEPTH, D_MODEL), 0.02),
        'ln_b': nrm(ks[24], (DEPTH, D_MODEL), 0.02),
    }


def reference(x_prompt, x_sample, state_rwkv, state_shift, state_conv, cache_kv_w128, cache_kv_w512,
              cache_kv_w2048, even_w_in, even_w_out, rwkv_mu, rwkv_w0, rwkv_w2, rwkv_a0, rwkv_a2,
              rwkv_k_k, rwkv_k_a, rwkv_r_k, rwkv_gn_g, rwkv_gn_b, conv_w, odd_w_in, odd_w_out, ln_g, ln_b):
    xp, xs = x_prompt, x_sample
    Bp = xp.shape[0]
    caches = (cache_kv_w128, cache_kv_w512, cache_kv_w2048)
    rwkv_p, rwkv_s, shift_p, shift_s, conv_p, conv_s = [], [], [], [], [], []
    kv_p = [[] for _ in range(N_C_GROUPS)]
    kv_s = [[] for _ in range(N_C_GROUPS)]
    for li in range(DEPTH):
        j = li // 2
        if li % 2 == 0:
            rw = (rwkv_mu[j], rwkv_w0[j], rwkv_w2[j], rwkv_a0[j], rwkv_a2[j], rwkv_k_k[j],
                  rwkv_k_a[j], rwkv_r_k[j], rwkv_gn_g[j], rwkv_gn_b[j])
            yp, sh_new, s_new, cv_new = even_mix(
                xp, jnp.zeros((Bp, A_SHIFT_W), xp.dtype),
                jnp.zeros((Bp, A_HEADS, A_HEAD_DIM, A_HEAD_DIM), jnp.float32),
                jnp.zeros((Bp, CONV_W - 1, B_WIDTH), xp.dtype),
                even_w_in[j], even_w_out[j], rw, conv_w[j])
            rwkv_p.append(s_new); shift_p.append(sh_new); conv_p.append(cv_new)
            ys, sh_new, s_new, cv_new = even_mix(
                xs, state_shift[j], state_rwkv[j], state_conv[j],
                even_w_in[j], even_w_out[j], rw, conv_w[j])
            rwkv_s.append(s_new); shift_s.append(sh_new); conv_s.append(cv_new)
        else:
            yp, bufs_p = odd_mix(xp, None, odd_w_in[j], odd_w_out[j])
            ys, bufs_s = odd_mix(xs, tuple(c[j] for c in caches), odd_w_in[j], odd_w_out[j])
            for g in range(N_C_GROUPS):
                kv_p[g].append(bufs_p[g])
                kv_s[g].append(bufs_s[g])
        xp = layer_norm(DEEPNORM_ALPHA * xp + yp, ln_g[li], ln_b[li])
        xs = layer_norm(DEEPNORM_ALPHA * xs + ys, ln_g[li], ln_b[li])
    return (xp, xs,
            jnp.stack(rwkv_p), jnp.stack(rwkv_s),
            jnp.stack(shift_p), jnp.stack(shift_s),
            jnp.stack(conv_p), jnp.stack(conv_s),
            jnp.stack(kv_p[0]), jnp.stack(kv_s[0]),
            jnp.stack(kv_p[1]), jnp.stack(kv_s[1]),
            jnp.stack(kv_p[2]), jnp.stack(kv_s[2]))
```

```python
import functools

import jax
import jax.numpy as jnp
from jax import lax
from jax.experimental import pallas as pl
from jax.experimental.pallas import tpu as pltpu

F32 = jnp.float32
BF16 = jnp.bfloat16
HIGHEST = lax.Precision.HIGHEST

D_MODEL = 1024
DEPTH = 4
HEAD = 64
LANES = 128
A_WIDTH = 512
N_PAIRS = A_WIDTH // LANES
A_SHIFT_W = 3 * A_WIDTH + 2 * HEAD
B_WIDTH = 512
EVEN_REST_W = A_WIDTH + 4 * B_WIDTH
C_GROUPS = ((128, 1), (512, 4), (2048, 16))
C_WIDTH = 512
C_QKV_W = 3 * len(C_GROUPS) * C_WIDTH
ODD_IN_W = C_QKV_W + C_WIDTH
ATTN_BLOCK = 128
GN_EPS = 64e-5
LN_EPS = 1e-5
DEEPNORM_ALPHA = (2 * DEPTH) ** 0.25
NEG_BIG = -1e30
WKV_CHUNK = 64
VMEM_LIMIT = 56 * 1024 * 1024


def _dot(a, b):
    return jnp.dot(a.astype(BF16), b.astype(BF16), preferred_element_type=F32)


def _dot_nt(a, b):
    return lax.dot_general(a.astype(BF16), b.astype(BF16), (((1,), (1,)), ((), ())),
                           preferred_element_type=F32)


def _dot_tn(a, b):
    return lax.dot_general(a.astype(BF16), b.astype(BF16), (((0,), (0,)), ((), ())),
                           preferred_element_type=F32)


def _dot_f32(a, b):
    return jnp.dot(a, b, precision=HIGHEST, preferred_element_type=F32)


def _silu(z):
    return z * (1.0 / (1.0 + jnp.exp(-z)))


def _layer_norm(xf, g, b):
    mu = jnp.mean(xf, axis=-1, keepdims=True)
    d = xf - mu
    var = jnp.mean(d * d, axis=-1, keepdims=True)
    return d * lax.rsqrt(var + LN_EPS) * g + b


def _full_spec(shape):
    return pl.BlockSpec(shape, lambda *_: (0,) * len(shape))


def _params(semantics):
    return pltpu.CompilerParams(dimension_semantics=semantics, vmem_limit_bytes=VMEM_LIMIT)


def _proj_kernel(n_out, x_ref, *refs):
    w_refs, o_refs = refs[:n_out], refs[n_out:]
    xb = x_ref[...].astype(BF16)
    for w_ref, o_ref in zip(w_refs, o_refs):
        o_ref[...] = jnp.dot(xb, w_ref[...], preferred_element_type=F32)


def _proj_call(x, weights, tm):
    B, T, D = x.shape
    M = B * T
    outs = pl.pallas_call(
        functools.partial(_proj_kernel, len(weights)),
        grid=(M // tm,),
        in_specs=[pl.BlockSpec((tm, D), lambda i: (i, 0))] + [_full_spec(w.shape) for w in weights],
        out_specs=[pl.BlockSpec((tm, w.shape[1]), lambda i: (i, 0)) for w in weights],
        out_shape=[jax.ShapeDtypeStruct((M, w.shape[1]), F32) for w in weights],
        compiler_params=_params(("parallel",)),
        name="in_proj",
    )(x.reshape(M, D), *weights)
    return [o.reshape(B, T, -1) for o in outs]


def _proj_odd_prompt_kernel(tm, x_ref, w_ref, *refs):
    p_ref, kv_refs = refs[-4], refs[-3:]
    p_ref[0] = jnp.dot(x_ref[0].astype(BF16), w_ref[...], preferred_element_type=F32)
    for g, ((window, _), kv_ref) in enumerate(zip(C_GROUPS, kv_refs)):
        rows = min(window, tm)
        k0 = C_QKV_W // 3 + g * C_WIDTH
        v0 = 2 * C_QKV_W // 3 + g * C_WIDTH
        kv_ref[0, 0, :, 0:C_WIDTH] = p_ref[0, tm - rows:tm, k0:k0 + C_WIDTH]
        kv_ref[0, 0, :, C_WIDTH:2 * C_WIDTH] = p_ref[0, tm - rows:tm, v0:v0 + C_WIDTH]


def _proj_odd_prompt_call(x, w, layer_slot, kv_prev, tm=256):
    B, T, D = x.shape
    nt = T // tm
    n_odd = DEPTH // 2
    kv_specs, kv_shapes = [], []
    for window, _ in C_GROUPS:
        keep = min(window, T)
        rows = min(window, tm)
        kt = keep // rows

        def index_map(b, i, kt=kt, whole_tiles=(rows == tm)):
            return (layer_slot, b, jnp.maximum(i - (nt - kt), 0) if whole_tiles else 0, 0)

        kv_specs.append(pl.BlockSpec((1, 1, rows, 2 * C_WIDTH), index_map))
        kv_shapes.append(jax.ShapeDtypeStruct((n_odd, B, keep, 2 * C_WIDTH), F32))
    in_specs = [pl.BlockSpec((1, tm, D), lambda b, i: (b, i, 0)), _full_spec(w.shape)]
    args = [x, w]
    aliases = {}
    if kv_prev is not None:
        in_specs += [pl.BlockSpec(memory_space=pl.ANY)] * 3
        args += list(kv_prev)
        aliases = {2: 1, 3: 2, 4: 3}
    outs = pl.pallas_call(
        functools.partial(_proj_odd_prompt_kernel, tm),
        grid=(B, nt),
        in_specs=in_specs,
        out_specs=[pl.BlockSpec((1, tm, ODD_IN_W), lambda b, i: (b, i, 0))] + kv_specs,
        out_shape=[jax.ShapeDtypeStruct((B, T, ODD_IN_W), F32)] + kv_shapes,
        input_output_aliases=aliases,
        compiler_params=_params(("parallel", "arbitrary")),
        name="in_proj_attn_prompt",
    )(*args)
    return outs[0], outs[1:]


def _head_sum(x, ones2):
    xh = x.astype(BF16)
    xl = (x - xh.astype(F32)).astype(BF16)
    return jnp.dot(jnp.concatenate([xh, xl], axis=1), ones2, preferred_element_type=F32)


def _prefix_sum(x, n):
    row = lax.broadcasted_iota(jnp.int32, (n, 1), 0)
    s = 1
    while s < n:
        x = x + jnp.where(row >= s, pltpu.roll(x, s, axis=0), 0.0)
        s *= 2
    return x


def _wkv_kernel(C, NCH, p_ref, pprev_ref, shift0_ref, s0_ref, mu_ref, w0_ref, w2p_ref, a0_ref, a2p_ref,
                kk_ref, ka_ref, rk_ref, gng_ref, gnb_ref,
                y_ref, sout_ref,
                r_s, k_s, v_s, al_s, be_s, lw_s, bo_s, y_s, st_s):
    tb = pl.program_id(1)
    TB = C * NCH

    @pl.when(tb == 0)
    def _():
        st_s[...] = s0_ref[0]

    p = p_ref[0]
    prevrow = jnp.where(tb == 0, shift0_ref[0], pprev_ref[0, 7:8, :])
    row = lax.broadcasted_iota(jnp.int32, (TB, 1), 0)
    prev = jnp.where(row == 0, prevrow, pltpu.roll(p, 1, axis=0))
    ps = p + (prev - p) * mu_ref[...]
    r = ps[:, 0:A_WIDTH]
    k = ps[:, A_WIDTH:2 * A_WIDTH]
    v = ps[:, 2 * A_WIDTH:3 * A_WIDTH]
    lora_in = ps[:, 3 * A_WIDTH:A_SHIFT_W]
    nw = -(w0_ref[...] + _dot_f32(jnp.tanh(lora_in), w2p_ref[...]))
    softplus = jnp.maximum(nw, 0.0) + jnp.log(1.0 + jnp.exp(-jnp.abs(nw)))
    lw = -jnp.exp(-softplus - 0.5)
    a = 1.0 / (1.0 + jnp.exp(-(a0_ref[...] + _dot_f32(lora_in, a2p_ref[...]))))

    li = lax.broadcasted_iota(jnp.int32, (2 * LANES, LANES), 0)
    lj = lax.broadcasted_iota(jnp.int32, (2 * LANES, LANES), 1)
    ones2 = jnp.where(((li % LANES) // HEAD) == (lj // HEAD), 1.0, 0.0).astype(BF16)

    kk = k * kk_ref[...]
    kmod = k * (1.0 + (a - 1.0) * ka_ref[...])
    rk = r * kmod * rk_ref[...]
    sls = [slice(hp * LANES, (hp + 1) * LANES) for hp in range(N_PAIRS)]
    for sl in sls:
        kkp = kk[:, sl]
        kkn = kkp * lax.rsqrt(jnp.maximum(_head_sum(kkp * kkp, ones2), 1e-24))
        al_s[:, sl] = -kkn
        be_s[:, sl] = kkn * a[:, sl]
        bo_s[:, sl] = _head_sum(rk[:, sl], ones2) * v[:, sl]
    r_s[...] = r
    k_s[...] = kmod
    v_s[...] = v
    lw_s[...] = lw

    ti = lax.broadcasted_iota(jnp.int32, (C, C), 0)
    tj = lax.broadcasted_iota(jnp.int32, (C, C), 1)
    strict = ti > tj
    incl = ti >= tj
    eye = jnp.where(ti == tj, 1.0, 0.0).astype(F32)
    level_masks = []
    b = 1
    while b < C:
        level_masks.append(((ti // (2 * b)) == (tj // (2 * b))) & (((ti // b) % 2) == 1) & (((tj // b) % 2) == 0))
        b *= 2
    lane = lax.broadcasted_iota(jnp.int32, (1, LANES), 1)
    head_masks = (lane < HEAD, lane >= HEAD)
    m0 = head_masks[0]
    bi = lax.broadcasted_iota(jnp.int32, (LANES, LANES), 0)
    bj = lax.broadcasted_iota(jnp.int32, (LANES, LANES), 1)
    pair_diag = (bi // HEAD) == (bj // HEAD)
    heads = [(hp, h) for hp in range(N_PAIRS) for h in range(2)]

    def chunk(c, carry):
        rows = pl.ds(pl.multiple_of(c * C, C), C)
        lw_c = lw_s[rows, :]
        linc = _prefix_sum(lw_c, C)
        lm = linc[C // 2 - 1:C // 2, :]
        lend = linc[C - 1:C, :]
        e_fwd = jnp.exp(linc - lm)
        e_bwd = jnp.exp(lm - linc)
        at = al_s[rows, :] * jnp.exp(linc - lw_c - lm)
        rt = r_s[rows, :] * e_fwd
        bt = be_s[rows, :] * e_bwd
        kt = k_s[rows, :] * e_bwd
        vv = v_s[rows, :]
        em = jnp.exp(lm)
        g = jnp.exp(lend - lm)
        pc = jnp.exp(lend)
        gb, gk = [], []
        for hp, h in heads:
            x = jnp.concatenate([jnp.where(head_masks[h], at[:, sls[hp]], 0.0),
                                 jnp.where(head_masks[h], rt[:, sls[hp]], 0.0)], axis=0)
            gb.append(_dot_nt(x, bt[:, sls[hp]]))
            gk.append(_dot_nt(x, kt[:, sls[hp]]))
        aab = [jnp.where(strict, m[:C], 0.0) for m in gb]
        aak = [jnp.where(strict, m[:C], 0.0) for m in gk]
        arb = [jnp.where(incl, m[C:], 0.0) for m in gb]
        ark = [jnp.where(incl, m[C:], 0.0) for m in gk]
        tinv = [eye + jnp.where(level_masks[0], n, 0.0) for n in aab]
        for lmask in level_masks[1:]:
            tmp = [_dot(t, jnp.where(lmask, n, 0.0)) for t, n in zip(tinv, aab)]
            tinv = [t + _dot(x, t) for t, x in zip(tinv, tmp)]
        av = [_dot(aak[i], vv[:, sls[hp]]) for i, (hp, h) in enumerate(heads)]
        zz = [jnp.concatenate([at[:, sls[hp]], jnp.where(m0, av[2 * hp], av[2 * hp + 1])], axis=1)
              for hp in range(N_PAIRS)]
        tz = [_dot(tinv[i], zz[hp]) for i, (hp, h) in enumerate(heads)]
        ahat = [jnp.where(m0, tz[2 * hp][:, :LANES], tz[2 * hp + 1][:, :LANES]) for hp in range(N_PAIRS)]
        wmat = [jnp.where(m0, tz[2 * hp][:, LANES:], tz[2 * hp + 1][:, LANES:]) for hp in range(N_PAIRS)]
        aw = [jnp.concatenate([ahat[hp], wmat[hp]], axis=1) for hp in range(N_PAIRS)]
        ry = [_dot(arb[i], aw[hp]) for i, (hp, h) in enumerate(heads)]
        akv = [_dot(ark[i], vv[:, sls[hp]]) for i, (hp, h) in enumerate(heads)]
        for hp in range(N_PAIRS):
            sl = sls[hp]
            rt_p, bt_p, kt_p, v_p = rt[:, sl], bt[:, sl], kt[:, sl], vv[:, sl]
            rhat = rt_p + jnp.where(m0, ry[2 * hp][:, :LANES], ry[2 * hp + 1][:, :LANES])
            yi = jnp.where(m0, ry[2 * hp][:, LANES:] + akv[2 * hp], ry[2 * hp + 1][:, LANES:] + akv[2 * hp + 1])
            em_p, g_p, pc_p = em[:, sl], g[:, sl], pc[:, sl]
            bg = bt_p * g_p
            kg = kt_p * g_p
            s_bd = st_s[hp]
            y_s[rows, sl] = _dot_nt(rhat * em_p, s_bd) + yi
            mp = jnp.where(pair_diag, _dot_tn(ahat[hp] * em_p, bg), 0.0)
            np_ = jnp.where(pair_diag, _dot_tn(jnp.concatenate([wmat[hp], v_p], axis=0),
                                               jnp.concatenate([bg, kg], axis=0)), 0.0)
            st_s[hp] = s_bd * pc_p + _dot(s_bd, mp) + np_
        return carry

    lax.fori_loop(0, NCH, chunk, 0)

    for sl in sls:
        y = y_s[:, sl]
        mean = _head_sum(y, ones2) * (1.0 / HEAD)
        d = y - mean
        var = _head_sum(d * d, ones2) * (1.0 / HEAD)
        y_ref[0, :, sl] = d * lax.rsqrt(var + GN_EPS) * gng_ref[:, sl] + gnb_ref[:, sl] + bo_s[:, sl]

    @pl.when(tb == pl.num_programs(1) - 1)
    def _():
        sout_ref[0] = st_s[...]


def _wkv_call(p, shift0, s0_bd, rw, *, C, TB):
    B, T, _ = p.shape
    return pl.pallas_call(
        functools.partial(_wkv_kernel, C, TB // C),
        grid=(B, T // TB),
        in_specs=[
            pl.BlockSpec((1, TB, A_SHIFT_W), lambda b, t: (b, t, 0)),
            pl.BlockSpec((1, 8, A_SHIFT_W), lambda b, t: (b, jnp.maximum(t * (TB // 8) - 1, 0), 0)),
            pl.BlockSpec((1, 1, A_SHIFT_W), lambda b, t: (b, 0, 0)),
            pl.BlockSpec((1, N_PAIRS, LANES, LANES), lambda b, t: (b, 0, 0, 0)),
        ] + [_full_spec(x.shape) for x in rw],
        out_specs=[
            pl.BlockSpec((1, TB, A_WIDTH), lambda b, t: (b, t, 0)),
            pl.BlockSpec((1, N_PAIRS, LANES, LANES), lambda b, t: (b, 0, 0, 0)),
        ],
        out_shape=[jax.ShapeDtypeStruct((B, T, A_WIDTH), F32),
                   jax.ShapeDtypeStruct((B, N_PAIRS, LANES, LANES), F32)],
        scratch_shapes=[pltpu.VMEM((TB, A_WIDTH), F32) for _ in range(8)]
        + [pltpu.VMEM((N_PAIRS, LANES, LANES), F32)],
        compiler_params=_params(("parallel", "arbitrary")),
        name="wkv7_chunked",
    )(p, p, shift0, s0_bd, *rw)


def _states_to_blockdiag(s):
    B = s.shape[0]
    s = s.reshape(B, N_PAIRS, 2, HEAD, 1, HEAD)
    sel = (jnp.arange(2)[:, None] == jnp.arange(2)[None, :]).reshape(1, 1, 2, 1, 2, 1)
    return jnp.where(sel, s, 0.0).reshape(B, N_PAIRS, LANES, LANES)


def _blockdiag_to_states(s_bd):
    B = s_bd.shape[0]
    s = s_bd.reshape(B, N_PAIRS, 2, HEAD, 2, HEAD)
    return jnp.stack([s[:, :, 0, :, 0, :], s[:, :, 1, :, 1, :]], axis=2).reshape(B, 2 * N_PAIRS, HEAD, HEAD)


def _even_out_kernel(ya_ref, pb_ref, pbprev_ref, u0_ref, x_ref, wout_ref, cw_ref, g_ref, b_ref,
                     o_ref, ulast_ref):
    tb = pl.program_id(1)
    TB = ya_ref.shape[1]
    pb = pb_ref[0]
    za, bb, cc, hh, zb = (pb[:, i * 512:(i + 1) * 512] for i in range(5))
    u = cc * hh
    pprev = pbprev_ref[0]
    uprev = jnp.where(tb == 0, u0_ref[0], pprev[:, 1024:1536] * pprev[:, 1536:2048])
    row = lax.broadcasted_iota(jnp.int32, (TB, 1), 0)
    u1 = jnp.where(row == 0, uprev[7:8], pltpu.roll(u, 1, axis=0))
    u2 = jnp.where(row == 0, uprev[6:7], jnp.where(row == 1, uprev[7:8], pltpu.roll(u, 2, axis=0)))
    cw = cw_ref[...]
    yb = bb * (cw[0:1] * u2 + cw[1:2] * u1 + cw[2:3] * u)
    mix_a = ya_ref[0] * _silu(za)
    mix_b = yb * _silu(zb)
    y = _dot(mix_a, wout_ref[0:A_WIDTH, :]) + _dot(mix_b, wout_ref[A_WIDTH:, :])
    o_ref[0] = _layer_norm(DEEPNORM_ALPHA * x_ref[0] + y, g_ref[...], b_ref[...])
    ulast_ref[0] = u[TB - 8:TB]


def _even_out_call(ya, pb, u0, x, wout, cw, g, b, TB):
    B, T, _ = x.shape
    return pl.pallas_call(
        _even_out_kernel,
        grid=(B, T // TB),
        in_specs=[
            pl.BlockSpec((1, TB, A_WIDTH), lambda b_, t: (b_, t, 0)),
            pl.BlockSpec((1, TB, EVEN_REST_W), lambda b_, t: (b_, t, 0)),
            pl.BlockSpec((1, 8, EVEN_REST_W), lambda b_, t: (b_, jnp.maximum(t * (TB // 8) - 1, 0), 0)),
            pl.BlockSpec((1, 8, B_WIDTH), lambda b_, t: (b_, 0, 0)),
            pl.BlockSpec((1, TB, D_MODEL), lambda b_, t: (b_, t, 0)),
            _full_spec(wout.shape), _full_spec(cw.shape), _full_spec(g.shape), _full_spec(b.shape),
        ],
        out_specs=[pl.BlockSpec((1, TB, D_MODEL), lambda b_, t: (b_, t, 0)),
                   pl.BlockSpec((1, 8, B_WIDTH), lambda b_, t: (b_, 0, 0))],
        out_shape=[jax.ShapeDtypeStruct((B, T, D_MODEL), F32), jax.ShapeDtypeStruct((B, 8, B_WIDTH), F32)],
        compiler_params=_params(("parallel", "arbitrary")),
        name="even_out",
    )(ya, pb, pb, u0, x, wout, cw, g, b)


def _attn_prompt_kernel(*refs):
    q_refs, k_refs, v_refs = refs[0:3], refs[3:6], refs[6:9]
    o_ref, og_s, lse_s = refs[9], refs[10], refs[11]
    BLK = ATTN_BLOCK
    lane = lax.broadcasted_iota(jnp.int32, (1, LANES), 1)
    m0 = lane < HEAD
    qi = lax.broadcasted_iota(jnp.int32, (2 * BLK, 2 * BLK), 0) % BLK
    kj = lax.broadcasted_iota(jnp.int32, (2 * BLK, 2 * BLK), 1)
    band = (kj >= qi) & (kj <= qi + BLK)
    qi1 = lax.broadcasted_iota(jnp.int32, (2 * BLK, BLK), 0) % BLK
    kj1 = lax.broadcasted_iota(jnp.int32, (2 * BLK, BLK), 1)
    causal = kj1 <= qi1
    for g, (window, d) in enumerate(C_GROUPS):
        assert window // d == BLK
        nb = q_refs[g].shape[1] // (d * BLK)
        for rho in range(d):
            k_prev = v_prev = None
            for i in range(nb):
                start = rho + d * BLK * i
                rows = pl.ds(start, BLK, stride=d) if d > 1 else pl.ds(start, BLK)
                q = q_refs[g][0, rows, :] * (HEAD ** -0.5)
                k_cur = k_refs[g][0, rows, :].astype(BF16)
                v_cur = v_refs[g][0, rows, :].astype(BF16)
                if i == 0:
                    keys, vals, mask = k_cur, v_cur, causal
                else:
                    keys = jnp.concatenate([k_prev, k_cur], axis=0)
                    vals = jnp.concatenate([v_prev, v_cur], axis=0)
                    mask = band
                q2 = jnp.concatenate([jnp.where(m0, q, 0.0), jnp.where(m0, 0.0, q)], axis=0)
                s = jnp.where(mask, _dot_nt(q2, keys), NEG_BIG)
                mx = jnp.max(s, axis=-1, keepdims=True)
                e = jnp.exp(s - mx)
                den = jnp.sum(e, axis=-1, keepdims=True)
                o2 = _dot(e, vals) * (1.0 / den)
                lse2 = mx + jnp.log(den)
                og_s[g, rows, :] = jnp.where(m0, o2[:BLK], o2[BLK:])
                lse_s[g, rows, :] = jnp.where(m0, lse2[:BLK], lse2[BLK:])
                k_prev, v_prev = k_cur, v_cur
    l0, l1, l2 = lse_s[0], lse_s[1], lse_s[2]
    mx = jnp.maximum(jnp.maximum(l0, l1), l2)
    w0, w1, w2 = jnp.exp(l0 - mx), jnp.exp(l1 - mx), jnp.exp(l2 - mx)
    o_ref[0] = (w0 * og_s[0] + w1 * og_s[1] + w2 * og_s[2]) * (1.0 / (w0 + w1 + w2))


def _attn_prompt_call(p):
    B, T, _ = p.shape
    n_g = len(C_GROUPS)

    def col_spec(part, g):
        base = (part * n_g + g) * C_WIDTH // LANES
        return pl.BlockSpec((1, T, LANES), lambda b, hp: (b, 0, base + hp))

    specs = [col_spec(part, g) for part in range(3) for g in range(n_g)]
    return pl.pallas_call(
        _attn_prompt_kernel,
        grid=(B, N_PAIRS),
        in_specs=specs,
        out_specs=pl.BlockSpec((1, T, LANES), lambda b, hp: (b, 0, hp)),
        out_shape=jax.ShapeDtypeStruct((B, T, C_WIDTH), F32),
        scratch_shapes=[pltpu.VMEM((n_g, T, LANES), F32), pltpu.VMEM((n_g, T, LANES), F32)],
        compiler_params=_params(("parallel", "parallel")),
        name="attn_prompt",
    )(*([p] * 9))


def _attn_sample_kernel(n_alias, ps_ref, *refs):
    c_refs = refs[0:3]
    o_ref = refs[3 + n_alias]
    kv_refs = refs[4 + n_alias:7 + n_alias]
    T = ps_ref.shape[1]
    p = ps_ref[0]
    lane = lax.broadcasted_iota(jnp.int32, (1, LANES), 1)
    m0 = lane < HEAD
    tq = lax.broadcasted_iota(jnp.int32, (2 * T, 1), 0) % T
    n_g = len(C_GROUPS)
    for hp in range(N_PAIRS):
        m_run = jnp.full((2 * T, 1), NEG_BIG, F32)
        l_run = jnp.zeros((2 * T, 1), F32)
        acc = jnp.zeros((2 * T, LANES), F32)
        for g, (window, d) in enumerate(C_GROUPS):
            wb = c_refs[g].shape[2]
            assert wb == window and window % d == 0 and wb >= (window // d) * d
            col = g * C_WIDTH + hp * LANES
            q = p[:, col:col + LANES] * (HEAD ** -0.5)
            k_new = p[:, C_QKV_W // 3 + col:C_QKV_W // 3 + col + LANES]
            v_new = p[:, 2 * C_QKV_W // 3 + col:2 * C_QKV_W // 3 + col + LANES]
            q2 = jnp.concatenate([jnp.where(m0, q, 0.0), jnp.where(m0, 0.0, q)], axis=0)
            k_old = c_refs[g][0, 0, :, hp * LANES:(hp + 1) * LANES]
            v_old = c_refs[g][0, 0, :, C_WIDTH + hp * LANES:C_WIDTH + (hp + 1) * LANES]
            pos = lax.broadcasted_iota(jnp.int32, (2 * T, wb), 1)
            ok_old = (pos >= tq + (wb - window)) & (((pos - tq) & (d - 1)) == 0)
            tn = lax.broadcasted_iota(jnp.int32, (2 * T, T), 1)
            ok_new = (tn <= tq) & (((tq - tn) & (d - 1)) == 0)
            s_old = jnp.where(ok_old, _dot_nt(q2, k_old), NEG_BIG)
            s_new = jnp.where(ok_new, _dot_nt(q2, k_new), NEG_BIG)
            m_new = jnp.maximum(m_run, jnp.maximum(jnp.max(s_old, axis=-1, keepdims=True),
                                                   jnp.max(s_new, axis=-1, keepdims=True)))
            e_old = jnp.exp(s_old - m_new)
            e_new = jnp.exp(s_new - m_new)
            scale = jnp.exp(m_run - m_new)
            l_run = l_run * scale + jnp.sum(e_old, axis=-1, keepdims=True) + jnp.sum(e_new, axis=-1, keepdims=True)
            acc = acc * scale + _dot(e_old, v_old) + _dot(e_new, v_new)
            m_run = m_new
        o2 = acc * (1.0 / l_run)
        o_ref[0, :, hp * LANES:(hp + 1) * LANES] = jnp.where(m0, o2[:T], o2[T:])
    for g in range(n_g):
        wb = c_refs[g].shape[2]
        kv_refs[g][0, 0, 0:wb - T, :] = c_refs[g][0, 0, T:wb, :]
        kv_refs[g][0, 0, wb - T:wb, 0:C_WIDTH] = p[:, C_QKV_W // 3 + g * C_WIDTH:C_QKV_W // 3 + (g + 1) * C_WIDTH]
        kv_refs[g][0, 0, wb - T:wb, C_WIDTH:] = p[:, 2 * C_QKV_W // 3 + g * C_WIDTH:2 * C_QKV_W // 3 + (g + 1) * C_WIDTH]


def _attn_sample_call(p, caches, layer_slot, kv_prev):
    B, T, _ = p.shape
    in_specs = [pl.BlockSpec((1, T, ODD_IN_W), lambda b: (b, 0, 0))]
    in_specs += [pl.BlockSpec((1, 1, c.shape[2], c.shape[3]), lambda b: (layer_slot, b, 0, 0)) for c in caches]
    args = [p] + list(caches)
    aliases = {}
    n_alias = 0
    if kv_prev is not None:
        n_alias = 3
        in_specs += [pl.BlockSpec(memory_space=pl.ANY)] * 3
        args += list(kv_prev)
        aliases = {4: 1, 5: 2, 6: 3}
    outs = pl.pallas_call(
        functools.partial(_attn_sample_kernel, n_alias),
        grid=(B,),
        in_specs=in_specs,
        out_specs=[pl.BlockSpec((1, T, C_WIDTH), lambda b: (b, 0, 0))]
        + [pl.BlockSpec((1, 1, c.shape[2], c.shape[3]), lambda b: (layer_slot, b, 0, 0)) for c in caches],
        out_shape=[jax.ShapeDtypeStruct((B, T, C_WIDTH), F32)]
        + [jax.ShapeDtypeStruct(c.shape, F32) for c in caches],
        input_output_aliases=aliases,
        compiler_params=_params(("parallel",)),
        name="attn_sample",
    )(*args)
    return outs[0], outs[1:]


def _odd_out_kernel(o_ref, z_ref, x_ref, wout_ref, g_ref, b_ref, out_ref):
    y = _dot(o_ref[...] * _silu(z_ref[...]), wout_ref[...])
    out_ref[...] = _layer_norm(DEEPNORM_ALPHA * x_ref[...] + y, g_ref[...], b_ref[...])


def _odd_out_call(o, p, x, wout, g, b, tm):
    B, T, D = x.shape
    M = B * T
    out = pl.pallas_call(
        _odd_out_kernel,
        grid=(M // tm,),
        in_specs=[
            pl.BlockSpec((tm, C_WIDTH), lambda i: (i, 0)),
            pl.BlockSpec((tm, C_WIDTH), lambda i: (i, C_QKV_W // C_WIDTH)),
            pl.BlockSpec((tm, D), lambda i: (i, 0)),
            _full_spec(wout.shape), _full_spec(g.shape), _full_spec(b.shape),
        ],
        out_specs=pl.BlockSpec((tm, D), lambda i: (i, 0)),
        out_shape=jax.ShapeDtypeStruct((M, D), F32),
        compiler_params=_params(("parallel",)),
        name="odd_out",
    )(o.reshape(M, C_WIDTH), p.reshape(M, ODD_IN_W), x.reshape(M, D), wout, g, b)
    return out.reshape(B, T, D)


def kernel(x_prompt, x_sample, state_rwkv, state_shift, state_conv, cache_kv_w128, cache_kv_w512, cache_kv_w2048, even_w_in, even_w_out, rwkv_mu, rwkv_w0, rwkv_w2, rwkv_a0, rwkv_a2, rwkv_k_k, rwkv_k_a, rwkv_r_k, rwkv_gn_g, rwkv_gn_b, conv_w, odd_w_in, odd_w_out, ln_g, ln_b):
    xp, xs = x_prompt, x_sample
    BP, TP, _ = xp.shape
    BS, TS, _ = xs.shape
    row = lambda t: t.reshape(1, -1)
    caches = [c.reshape(c.shape[0], c.shape[1], c.shape[2], 2 * C_WIDTH)
              for c in (cache_kv_w128, cache_kv_w512, cache_kv_w2048)]
    rwkv_p, rwkv_s, shift_p, shift_s, conv_p, conv_s = [], [], [], [], [], []
    kv_p = kv_s = None
    for li in range(DEPTH):
        j = li // 2
        g, b = row(ln_g[li]), row(ln_b[li])
        if li % 2 == 0:
            w_in = even_w_in[j].astype(BF16)
            w_a, w_rest = w_in[:, :A_SHIFT_W], w_in[:, A_SHIFT_W:]
            w_out = even_w_out[j].astype(BF16)
            zeros_lora = jnp.zeros((HEAD, A_WIDTH), F32)
            rw = (row(rwkv_mu[j]), row(rwkv_w0[j]), jnp.concatenate([rwkv_w2[j], zeros_lora], 0),
                  row(rwkv_a0[j]), jnp.concatenate([zeros_lora, rwkv_a2[j]], 0), row(rwkv_k_k[j]),
                  row(rwkv_k_a[j]), row(rwkv_r_k[j]), row(rwkv_gn_g[j]), row(rwkv_gn_b[j]))
            pa, pb = _proj_call(xp, [w_a, w_rest], tm=512)
            ya, s_bd = _wkv_call(pa, jnp.zeros((BP, 1, A_SHIFT_W), F32),
                                 jnp.zeros((BP, N_PAIRS, LANES, LANES), F32), rw, C=WKV_CHUNK, TB=4 * WKV_CHUNK)
            xp, ulast = _even_out_call(ya, pb, jnp.zeros((BP, 8, B_WIDTH), F32), xp, w_out, conv_w[j], g, b,
                                       TB=4 * WKV_CHUNK)
            rwkv_p.append(_blockdiag_to_states(s_bd))
            shift_p.append(pa[:, -1, :])
            conv_p.append(ulast[:, 6:8, :])
            pa, pb = _proj_call(xs, [w_a, w_rest], tm=BS * TS)
            ya, s_bd = _wkv_call(pa, state_shift[j].reshape(BS, 1, A_SHIFT_W),
                                 _states_to_blockdiag(state_rwkv[j]), rw, C=TS, TB=TS)
            u0 = jnp.concatenate([jnp.zeros((BS, 6, B_WIDTH), F32), state_conv[j]], axis=1)
            xs, ulast = _even_out_call(ya, pb, u0, xs, w_out, conv_w[j], g, b, TB=TS)
            rwkv_s.append(_blockdiag_to_states(s_bd))
            shift_s.append(pa[:, -1, :])
            conv_s.append(ulast[:, 6:8, :])
        else:
            w_in = odd_w_in[j].astype(BF16)
            w_out = odd_w_out[j].astype(BF16)
            pp, kv_p = _proj_odd_prompt_call(xp, w_in, j, kv_p)
            op = _attn_prompt_call(pp)
            xp = _odd_out_call(op, pp, xp, w_out, g, b, tm=512)
            (ps,) = _proj_call(xs, [w_in], tm=BS * TS)
            os_, kv_s = _attn_sample_call(ps, caches, j, kv_s)
            xs = _odd_out_call(os_, ps, xs, w_out, g, b, tm=BS * TS)
    kv_shape = lambda t: t.reshape(t.shape[0], t.shape[1], t.shape[2], 2, C_WIDTH // HEAD, HEAD)
    return (xp, xs,
            jnp.stack(rwkv_p), jnp.stack(rwkv_s),
            jnp.stack(shift_p), jnp.stack(shift_s),
            jnp.stack(conv_p), jnp.stack(conv_s),
            kv_shape(kv_p[0]), kv_shape(kv_s[0]),
            kv_shape(kv_p[1]), kv_shape(kv_s[1]),
            kv_shape(kv_p[2]), kv_shape(kv_s[2]))
```

```python
import functools

import jax
import jax.numpy as jnp
from jax import lax
from jax.experimental import pallas as pl
from jax.experimental.pallas import tpu as pltpu

F32 = jnp.float32
BF16 = jnp.bfloat16
HIGHEST = lax.Precision.HIGHEST

D_MODEL = 1024
DEPTH = 4
HEAD = 64
LANES = 128
A_WIDTH = 512
N_PAIRS = A_WIDTH // LANES
A_SHIFT_W = 3 * A_WIDTH + 2 * HEAD
B_WIDTH = 512
EVEN_REST_W = A_WIDTH + 4 * B_WIDTH
C_GROUPS = ((128, 1), (512, 4), (2048, 16))
C_WIDTH = 512
C_QKV_W = 3 * len(C_GROUPS) * C_WIDTH
ODD_IN_W = C_QKV_W + C_WIDTH
ATTN_BLOCK = 128
GN_EPS = 64e-5
LN_EPS = 1e-5
DEEPNORM_ALPHA = (2 * DEPTH) ** 0.25
NEG_BIG = -1e30
WKV_CHUNK = 64
VMEM_LIMIT = 56 * 1024 * 1024


def _dot(a, b):
    return jnp.dot(a.astype(BF16), b.astype(BF16), preferred_element_type=F32)


def _dot_nt(a, b):
    return lax.dot_general(a.astype(BF16), b.astype(BF16), (((1,), (1,)), ((), ())),
                           preferred_element_type=F32)


def _dot_tn(a, b):
    return lax.dot_general(a.astype(BF16), b.astype(BF16), (((0,), (0,)), ((), ())),
                           preferred_element_type=F32)


def _dot_f32(a, b):
    return jnp.dot(a, b, precision=HIGHEST, preferred_element_type=F32)


def _split_bf16(x):
    hi = x.astype(BF16)
    return hi, (x - hi.astype(F32)).astype(BF16)


def _dot_split(a, b3):
    hi, lo = _split_bf16(a)
    return jnp.dot(jnp.concatenate([hi, hi, lo], axis=1), b3, preferred_element_type=F32)


def _stack_split(b):
    hi, lo = _split_bf16(b)
    return jnp.concatenate([hi, lo, hi], axis=0)


def _silu(z):
    return z * (1.0 / (1.0 + jnp.exp(-z)))


def _layer_norm(xf, g, b):
    mu = jnp.mean(xf, axis=-1, keepdims=True)
    d = xf - mu
    var = jnp.mean(d * d, axis=-1, keepdims=True)
    return d * lax.rsqrt(var + LN_EPS) * g + b


def _full_spec(shape):
    return pl.BlockSpec(shape, lambda *_: (0,) * len(shape))


def _params(semantics):
    return pltpu.CompilerParams(dimension_semantics=semantics, vmem_limit_bytes=VMEM_LIMIT)


def _proj_kernel(n_out, x_ref, *refs):
    w_refs, o_refs = refs[:n_out], refs[n_out:]
    xb = x_ref[...].astype(BF16)
    for w_ref, o_ref in zip(w_refs, o_refs):
        o_ref[...] = jnp.dot(xb, w_ref[...], preferred_element_type=F32)


def _proj_call(x, weights, tm):
    B, T, D = x.shape
    M = B * T
    outs = pl.pallas_call(
        functools.partial(_proj_kernel, len(weights)),
        grid=(M // tm,),
        in_specs=[pl.BlockSpec((tm, D), lambda i: (i, 0))] + [_full_spec(w.shape) for w in weights],
        out_specs=[pl.BlockSpec((tm, w.shape[1]), lambda i: (i, 0)) for w in weights],
        out_shape=[jax.ShapeDtypeStruct((M, w.shape[1]), F32) for w in weights],
        compiler_params=_params(("parallel",)),
        name="in_proj",
    )(x.reshape(M, D), *weights)
    return [o.reshape(B, T, -1) for o in outs]


def _kept_rows(window, T, tm):
    rows = min(window, tm)
    return rows, min(window, T) // rows


def _proj_odd_prompt_kernel(tm, T, x_ref, w_ref, *refs):
    p_ref, kv_refs = refs[-4], refs[-3:]
    i = pl.program_id(1)
    nt = T // tm
    p_ref[0] = jnp.dot(x_ref[0].astype(BF16), w_ref[...], preferred_element_type=F32)
    for g, ((window, _), kv_ref) in enumerate(zip(C_GROUPS, kv_refs)):
        rows, kt = _kept_rows(window, T, tm)

        @pl.when(i >= nt - kt)
        def _():
            for part in range(2):
                c0 = (part + 1) * C_QKV_W // 3 + g * C_WIDTH
                kv_ref[0, 0, part] = p_ref[0, tm - rows:tm, c0:c0 + C_WIDTH].T


def _proj_odd_prompt_call(x, w, layer_slot, kv_prev, tm=256):
    B, T, D = x.shape
    nt = T // tm
    n_odd = DEPTH // 2
    kv_specs, kv_shapes = [], []
    for window, _ in C_GROUPS:
        rows, kt = _kept_rows(window, T, tm)

        def index_map(b, i, kt=kt):
            return (layer_slot, b, 0, 0, jnp.maximum(i - (nt - kt), 0))

        kv_specs.append(pl.BlockSpec((1, 1, 2, C_WIDTH, rows), index_map))
        kv_shapes.append(jax.ShapeDtypeStruct((n_odd, B, 2, C_WIDTH, rows * kt), F32))
    in_specs = [pl.BlockSpec((1, tm, D), lambda b, i: (b, i, 0)), _full_spec(w.shape)]
    args = [x, w]
    aliases = {}
    if kv_prev is not None:
        in_specs += [pl.BlockSpec(memory_space=pl.ANY)] * 3
        args += list(kv_prev)
        aliases = {2: 1, 3: 2, 4: 3}
    outs = pl.pallas_call(
        functools.partial(_proj_odd_prompt_kernel, tm, T),
        grid=(B, nt),
        in_specs=in_specs,
        out_specs=[pl.BlockSpec((1, tm, ODD_IN_W), lambda b, i: (b, i, 0))] + kv_specs,
        out_shape=[jax.ShapeDtypeStruct((B, T, ODD_IN_W), F32)] + kv_shapes,
        input_output_aliases=aliases,
        compiler_params=_params(("parallel", "arbitrary")),
        name="in_proj_attn_prompt",
    )(*args)
    return outs[0], outs[1:]


def _head_sum(x, ones2):
    xh = x.astype(BF16)
    xl = (x - xh.astype(F32)).astype(BF16)
    return jnp.dot(jnp.concatenate([xh, xl], axis=1), ones2, preferred_element_type=F32)


def _prefix_sum(x, n):
    row = lax.broadcasted_iota(jnp.int32, (n, 1), 0)
    s = 1
    while s < n:
        x = x + jnp.where(row >= s, pltpu.roll(x, s, axis=0), 0.0)
        s *= 2
    return x


def _wkv_kernel(C, NCH, CPI, p_ref, pprev_ref, shift0_ref, s0_ref, mu_ref, w0_ref, w2p_ref, a0_ref, a2p_ref,
                kk_ref, ka_ref, rk_ref, gng_ref, gnb_ref,
                y_ref, sout_ref,
                r_s, k_s, v_s, al_s, be_s, lw_s, bo_s, y_s, st_s):
    tb = pl.program_id(1)
    TB = C * NCH

    ei = lax.broadcasted_iota(jnp.int32, (HEAD, LANES), 0)
    ej = lax.broadcasted_iota(jnp.int32, (HEAD, LANES), 1)
    place = [jnp.where(ej == ei + h * HEAD, 1.0, 0.0).astype(F32) for h in range(2)]

    @pl.when(tb == 0)
    def _():
        for hp in range(N_PAIRS):
            for h in range(2):
                st_s[hp, h * HEAD:(h + 1) * HEAD, :] = _dot_f32(s0_ref[0, 2 * hp + h], place[h])

    p = p_ref[0]
    prevrow = jnp.where(tb == 0, shift0_ref[0], pprev_ref[0, 7:8, :])
    row = lax.broadcasted_iota(jnp.int32, (TB, 1), 0)
    prev = jnp.where(row == 0, prevrow, pltpu.roll(p, 1, axis=0))
    ps = p + (prev - p) * mu_ref[...]
    r = ps[:, 0:A_WIDTH]
    k = ps[:, A_WIDTH:2 * A_WIDTH]
    v = ps[:, 2 * A_WIDTH:3 * A_WIDTH]
    lora_in = ps[:, 3 * A_WIDTH:A_SHIFT_W]
    nw = -(w0_ref[...] + _dot_split(jnp.tanh(lora_in), w2p_ref[...]))
    softplus = jnp.maximum(nw, 0.0) + jnp.log(1.0 + jnp.exp(-jnp.abs(nw)))
    lw = -jnp.exp(-softplus - 0.5)
    a = 1.0 / (1.0 + jnp.exp(-(a0_ref[...] + _dot_split(lora_in, a2p_ref[...]))))

    li = lax.broadcasted_iota(jnp.int32, (2 * LANES, LANES), 0)
    lj = lax.broadcasted_iota(jnp.int32, (2 * LANES, LANES), 1)
    ones2 = jnp.where(((li % LANES) // HEAD) == (lj // HEAD), 1.0, 0.0).astype(BF16)

    kk = k * kk_ref[...]
    kmod = k * (1.0 + (a - 1.0) * ka_ref[...])
    rk = r * kmod * rk_ref[...]
    sls = [slice(hp * LANES, (hp + 1) * LANES) for hp in range(N_PAIRS)]
    for sl in sls:
        kkp = kk[:, sl]
        kkn = kkp * lax.rsqrt(jnp.maximum(_head_sum(kkp * kkp, ones2), 1e-24))
        al_s[:, sl] = -kkn
        be_s[:, sl] = kkn * a[:, sl]
        bo_s[:, sl] = _head_sum(rk[:, sl], ones2) * v[:, sl]
    r_s[...] = r
    k_s[...] = kmod
    v_s[...] = v
    lw_s[...] = lw

    ti = lax.broadcasted_iota(jnp.int32, (C, C), 0)
    tj = lax.broadcasted_iota(jnp.int32, (C, C), 1)
    strict = ti > tj
    incl = ti >= tj
    eye = jnp.where(ti == tj, 1.0, 0.0).astype(F32)
    level_masks = []
    b = 1
    while b < C:
        level_masks.append(((ti // (2 * b)) == (tj // (2 * b))) & (((ti // b) % 2) == 1) & (((tj // b) % 2) == 0))
        b *= 2
    lane = lax.broadcasted_iota(jnp.int32, (1, LANES), 1)
    head_masks = (lane < HEAD, lane >= HEAD)
    m0 = head_masks[0]
    bi = lax.broadcasted_iota(jnp.int32, (LANES, LANES), 0)
    bj = lax.broadcasted_iota(jnp.int32, (LANES, LANES), 1)
    pair_diag = (bi // HEAD) == (bj // HEAD)
    heads = [(hp, h) for hp in range(N_PAIRS) for h in range(2)]

    def chunk_group(it, carry):
        cks = []
        for ci in range(CPI):
            rows = pl.ds(pl.multiple_of((it * CPI + ci) * C, C), C)
            lw_c = lw_s[rows, :]
            linc = _prefix_sum(lw_c, C)
            lm = linc[C // 2 - 1:C // 2, :]
            lend = linc[C - 1:C, :]
            e_fwd = jnp.exp(linc - lm)
            e_bwd = jnp.exp(lm - linc)
            cks.append(dict(
                rows=rows,
                at=al_s[rows, :] * jnp.exp(linc - lw_c - lm),
                rt=r_s[rows, :] * e_fwd,
                bt=be_s[rows, :] * e_bwd,
                kt=k_s[rows, :] * e_bwd,
                vv=v_s[rows, :],
                em=jnp.exp(lm), g=jnp.exp(lend - lm), pc=jnp.exp(lend)))
        units = [(ck, hp, h) for ck in cks for hp in range(N_PAIRS) for h in range(2)]
        pairs = [(ck, hp) for ck in cks for hp in range(N_PAIRS)]
        gb, gk = [], []
        for ck, hp, h in units:
            x = jnp.concatenate([jnp.where(head_masks[h], ck["at"][:, sls[hp]], 0.0),
                                 jnp.where(head_masks[h], ck["rt"][:, sls[hp]], 0.0)], axis=0)
            gb.append(_dot_nt(x, ck["bt"][:, sls[hp]]))
            gk.append(_dot_nt(x, ck["kt"][:, sls[hp]]))
        aab = [jnp.where(strict, m[:C], 0.0) for m in gb]
        aak = [jnp.where(strict, m[:C], 0.0) for m in gk]
        arb = [jnp.where(incl, m[C:], 0.0) for m in gb]
        ark = [jnp.where(incl, m[C:], 0.0) for m in gk]
        tinv = [eye + jnp.where(level_masks[0], n, 0.0) for n in aab]
        for lmask in level_masks[1:]:
            tmp = [_dot(t, jnp.where(lmask, n, 0.0)) for t, n in zip(tinv, aab)]
            tinv = [t + _dot(x, t) for t, x in zip(tinv, tmp)]
        av = [_dot(aak[i], ck["vv"][:, sls[hp]]) for i, (ck, hp, h) in enumerate(units)]
        zz = [jnp.concatenate([ck["at"][:, sls[hp]], jnp.where(m0, av[2 * j], av[2 * j + 1])], axis=1)
              for j, (ck, hp) in enumerate(pairs)]
        tz = [_dot(tinv[i], zz[i // 2]) for i in range(len(units))]
        ahat = [jnp.where(m0, tz[2 * j][:, :LANES], tz[2 * j + 1][:, :LANES]) for j in range(len(pairs))]
        wmat = [jnp.where(m0, tz[2 * j][:, LANES:], tz[2 * j + 1][:, LANES:]) for j in range(len(pairs))]
        aw = [jnp.concatenate([ahat[j], wmat[j]], axis=1) for j in range(len(pairs))]
        ry = [_dot(arb[i], aw[i // 2]) for i in range(len(units))]
        akv = [_dot(ark[i], ck["vv"][:, sls[hp]]) for i, (ck, hp, h) in enumerate(units)]
        rhat_e, yi, mp, np_ = [], [], [], []
        for j, (ck, hp) in enumerate(pairs):
            sl = sls[hp]
            em_p, g_p = ck["em"][:, sl], ck["g"][:, sl]
            bg = ck["bt"][:, sl] * g_p
            kg = ck["kt"][:, sl] * g_p
            rhat_e.append((ck["rt"][:, sl] + jnp.where(m0, ry[2 * j][:, :LANES], ry[2 * j + 1][:, :LANES])) * em_p)
            yi.append(jnp.where(m0, ry[2 * j][:, LANES:] + akv[2 * j], ry[2 * j + 1][:, LANES:] + akv[2 * j + 1]))
            mp.append(jnp.where(pair_diag, _dot_tn(ahat[j] * em_p, bg), 0.0))
            np_.append(jnp.where(pair_diag, _dot_tn(jnp.concatenate([wmat[j], ck["vv"][:, sl]], axis=0),
                                                    jnp.concatenate([bg, kg], axis=0)), 0.0))
        for hp in range(N_PAIRS):
            s_bd = st_s[hp]
            for ci, ck in enumerate(cks):
                j = ci * N_PAIRS + hp
                y_s[ck["rows"], sls[hp]] = _dot_nt(rhat_e[j], s_bd) + yi[j]
                s_bd = s_bd * ck["pc"][:, sls[hp]] + _dot(s_bd, mp[j]) + np_[j]
            st_s[hp] = s_bd
        return carry

    lax.fori_loop(0, NCH // CPI, chunk_group, 0)

    for sl in sls:
        y = y_s[:, sl]
        mean = _head_sum(y, ones2) * (1.0 / HEAD)
        d = y - mean
        var = _head_sum(d * d, ones2) * (1.0 / HEAD)
        y_ref[0, :, sl] = d * lax.rsqrt(var + GN_EPS) * gng_ref[:, sl] + gnb_ref[:, sl] + bo_s[:, sl]

    @pl.when(tb == pl.num_programs(1) - 1)
    def _():
        for hp in range(N_PAIRS):
            for h in range(2):
                sout_ref[0, 2 * hp + h] = lax.dot_general(
                    st_s[hp, h * HEAD:(h + 1) * HEAD, :], place[h], (((1,), (1,)), ((), ())),
                    precision=HIGHEST, preferred_element_type=F32)


def _wkv_call(p, shift0, s0, rw, *, C, TB, CPI):
    B, T, _ = p.shape
    H = 2 * N_PAIRS
    return pl.pallas_call(
        functools.partial(_wkv_kernel, C, TB // C, CPI),
        grid=(B, T // TB),
        in_specs=[
            pl.BlockSpec((1, TB, A_SHIFT_W), lambda b, t: (b, t, 0)),
            pl.BlockSpec((1, 8, A_SHIFT_W), lambda b, t: (b, jnp.maximum(t * (TB // 8) - 1, 0), 0)),
            pl.BlockSpec((1, 1, A_SHIFT_W), lambda b, t: (b, 0, 0)),
            pl.BlockSpec((1, H, HEAD, HEAD), lambda b, t: (b, 0, 0, 0)),
        ] + [_full_spec(x.shape) for x in rw],
        out_specs=[
            pl.BlockSpec((1, TB, A_WIDTH), lambda b, t: (b, t, 0)),
            pl.BlockSpec((1, H, HEAD, HEAD), lambda b, t: (b, 0, 0, 0)),
        ],
        out_shape=[jax.ShapeDtypeStruct((B, T, A_WIDTH), F32),
                   jax.ShapeDtypeStruct((B, H, HEAD, HEAD), F32)],
        scratch_shapes=[pltpu.VMEM((TB, A_WIDTH), F32) for _ in range(8)]
        + [pltpu.VMEM((N_PAIRS, LANES, LANES), F32)],
        compiler_params=_params(("parallel", "arbitrary")),
        name="wkv7_chunked",
    )(p, p, shift0, s0, *rw)


def _even_out_kernel(ya_ref, pb_ref, pbprev_ref, u0_ref, x_ref, wout_ref, cw_ref, g_ref, b_ref,
                     o_ref, ulast_ref):
    tb = pl.program_id(1)
    TB = ya_ref.shape[1]
    pb = pb_ref[0]
    za, bb, cc, hh, zb = (pb[:, i * 512:(i + 1) * 512] for i in range(5))
    u = cc * hh
    pprev = pbprev_ref[0]
    uprev = jnp.where(tb == 0, u0_ref[0], pprev[:, 1024:1536] * pprev[:, 1536:2048])
    row = lax.broadcasted_iota(jnp.int32, (TB, 1), 0)
    u1 = jnp.where(row == 0, uprev[7:8], pltpu.roll(u, 1, axis=0))
    u2 = jnp.where(row == 0, uprev[6:7], jnp.where(row == 1, uprev[7:8], pltpu.roll(u, 2, axis=0)))
    cw = cw_ref[...]
    yb = bb * (cw[0:1] * u2 + cw[1:2] * u1 + cw[2:3] * u)
    mix_a = ya_ref[0] * _silu(za)
    mix_b = yb * _silu(zb)
    y = _dot(mix_a, wout_ref[0:A_WIDTH, :]) + _dot(mix_b, wout_ref[A_WIDTH:, :])
    o_ref[0] = _layer_norm(DEEPNORM_ALPHA * x_ref[0] + y, g_ref[...], b_ref[...])
    ulast_ref[0] = u[TB - 8:TB]


def _even_out_call(ya, pb, u0, x, wout, cw, g, b, TB):
    B, T, _ = x.shape
    return pl.pallas_call(
        _even_out_kernel,
        grid=(B, T // TB),
        in_specs=[
            pl.BlockSpec((1, TB, A_WIDTH), lambda b_, t: (b_, t, 0)),
            pl.BlockSpec((1, TB, EVEN_REST_W), lambda b_, t: (b_, t, 0)),
            pl.BlockSpec((1, 8, EVEN_REST_W), lambda b_, t: (b_, jnp.maximum(t * (TB // 8) - 1, 0), 0)),
            pl.BlockSpec((1, 8, B_WIDTH), lambda b_, t: (b_, 0, 0)),
            pl.BlockSpec((1, TB, D_MODEL), lambda b_, t: (b_, t, 0)),
            _full_spec(wout.shape), _full_spec(cw.shape), _full_spec(g.shape), _full_spec(b.shape),
        ],
        out_specs=[pl.BlockSpec((1, TB, D_MODEL), lambda b_, t: (b_, t, 0)),
                   pl.BlockSpec((1, 8, B_WIDTH), lambda b_, t: (b_, 0, 0))],
        out_shape=[jax.ShapeDtypeStruct((B, T, D_MODEL), F32), jax.ShapeDtypeStruct((B, 8, B_WIDTH), F32)],
        compiler_params=_params(("parallel", "arbitrary")),
        name="even_out",
    )(ya, pb, pb, u0, x, wout, cw, g, b)


def _attn_prompt_kernel(*refs):
    q_refs, k_refs, v_refs = refs[0:3], refs[3:6], refs[6:9]
    o_ref, og_s, lse_s = refs[9], refs[10], refs[11]
    BLK = ATTN_BLOCK
    lane = lax.broadcasted_iota(jnp.int32, (1, LANES), 1)
    m0 = lane < HEAD
    qi = lax.broadcasted_iota(jnp.int32, (2 * BLK, 2 * BLK), 0) % BLK
    kj = lax.broadcasted_iota(jnp.int32, (2 * BLK, 2 * BLK), 1)
    band = (kj >= qi) & (kj <= qi + BLK)
    qi1 = lax.broadcasted_iota(jnp.int32, (2 * BLK, BLK), 0) % BLK
    kj1 = lax.broadcasted_iota(jnp.int32, (2 * BLK, BLK), 1)
    causal = kj1 <= qi1
    for g, (window, d) in enumerate(C_GROUPS):
        assert window // d == BLK
        nb = q_refs[g].shape[1] // (d * BLK)
        for rho in range(d):
            k_prev = v_prev = None
            for i in range(nb):
                start = rho + d * BLK * i
                rows = pl.ds(start, BLK, stride=d) if d > 1 else pl.ds(start, BLK)
                q = q_refs[g][0, rows, :] * (HEAD ** -0.5)
                k_cur = k_refs[g][0, rows, :].astype(BF16)
                v_cur = v_refs[g][0, rows, :].astype(BF16)
                if i == 0:
                    keys, vals, mask = k_cur, v_cur, causal
                else:
                    keys = jnp.concatenate([k_prev, k_cur], axis=0)
                    vals = jnp.concatenate([v_prev, v_cur], axis=0)
                    mask = band
                q2 = jnp.concatenate([jnp.where(m0, q, 0.0), jnp.where(m0, 0.0, q)], axis=0)
                s = jnp.where(mask, _dot_nt(q2, keys), NEG_BIG)
                mx = jnp.max(s, axis=-1, keepdims=True)
                e = jnp.exp(s - mx)
                den = jnp.sum(e, axis=-1, keepdims=True)
                o2 = _dot(e, vals) * (1.0 / den)
                lse2 = mx + jnp.log(den)
                og_s[g, rows, :] = jnp.where(m0, o2[:BLK], o2[BLK:])
                lse_s[g, rows, :] = jnp.where(m0, lse2[:BLK], lse2[BLK:])
                k_prev, v_prev = k_cur, v_cur
    l0, l1, l2 = lse_s[0], lse_s[1], lse_s[2]
    mx = jnp.maximum(jnp.maximum(l0, l1), l2)
    w0, w1, w2 = jnp.exp(l0 - mx), jnp.exp(l1 - mx), jnp.exp(l2 - mx)
    o_ref[0] = (w0 * og_s[0] + w1 * og_s[1] + w2 * og_s[2]) * (1.0 / (w0 + w1 + w2))


def _attn_prompt_call(p):
    B, T, _ = p.shape
    n_g = len(C_GROUPS)

    def col_spec(part, g):
        base = (part * n_g + g) * C_WIDTH // LANES
        return pl.BlockSpec((1, T, LANES), lambda b, hp: (b, 0, base + hp))

    specs = [col_spec(part, g) for part in range(3) for g in range(n_g)]
    return pl.pallas_call(
        _attn_prompt_kernel,
        grid=(B, N_PAIRS),
        in_specs=specs,
        out_specs=pl.BlockSpec((1, T, LANES), lambda b, hp: (b, 0, hp)),
        out_shape=jax.ShapeDtypeStruct((B, T, C_WIDTH), F32),
        scratch_shapes=[pltpu.VMEM((n_g, T, LANES), F32), pltpu.VMEM((n_g, T, LANES), F32)],
        compiler_params=_params(("parallel", "parallel")),
        name="attn_prompt",
    )(*([p] * 9))


def _attn_sample_kernel(n_alias, ps_ref, *refs):
    c_refs = refs[0:3]
    o_ref = refs[3 + n_alias]
    kv_refs = refs[4 + n_alias:7 + n_alias]
    T = ps_ref.shape[1]
    p = ps_ref[0]
    lane = lax.broadcasted_iota(jnp.int32, (1, LANES), 1)
    m0 = lane < HEAD
    tq = lax.broadcasted_iota(jnp.int32, (2 * T, 1), 0) % T
    for hp in range(N_PAIRS):
        m_run = jnp.full((2 * T, 1), NEG_BIG, F32)
        l_run = jnp.zeros((2 * T, 1), F32)
        acc = jnp.zeros((2 * T, LANES), F32)
        ch = slice(hp * LANES, (hp + 1) * LANES)
        for g, (window, d) in enumerate(C_GROUPS):
            wb = c_refs[g].shape[4]
            assert wb == window and window % d == 0
            col = g * C_WIDTH + hp * LANES
            q = p[:, col:col + LANES] * (HEAD ** -0.5)
            k_new = p[:, C_QKV_W // 3 + col:C_QKV_W // 3 + col + LANES]
            v_new = p[:, 2 * C_QKV_W // 3 + col:2 * C_QKV_W // 3 + col + LANES]
            q2 = jnp.concatenate([jnp.where(m0, q, 0.0), jnp.where(m0, 0.0, q)], axis=0)
            pos = lax.broadcasted_iota(jnp.int32, (2 * T, wb), 1)
            ok_old = (pos >= tq + (wb - window)) & (((pos - tq) & (d - 1)) == 0)
            tn = lax.broadcasted_iota(jnp.int32, (2 * T, T), 1)
            ok_new = (tn <= tq) & (((tq - tn) & (d - 1)) == 0)
            s_old = jnp.where(ok_old, _dot(q2, c_refs[g][0, 0, 0, ch, :]), NEG_BIG)
            s_new = jnp.where(ok_new, _dot_nt(q2, k_new), NEG_BIG)
            m_new = jnp.maximum(m_run, jnp.maximum(jnp.max(s_old, axis=-1, keepdims=True),
                                                   jnp.max(s_new, axis=-1, keepdims=True)))
            e_old = jnp.exp(s_old - m_new)
            e_new = jnp.exp(s_new - m_new)
            scale = jnp.exp(m_run - m_new)
            l_run = l_run * scale + jnp.sum(e_old, axis=-1, keepdims=True) + jnp.sum(e_new, axis=-1, keepdims=True)
            acc = acc * scale + _dot_nt(e_old, c_refs[g][0, 0, 1, ch, :]) + _dot(e_new, v_new)
            m_run = m_new
        o2 = acc * (1.0 / l_run)
        o_ref[0, :, ch] = jnp.where(m0, o2[:T], o2[T:])
    tail = lane >= LANES - T
    for g in range(len(C_GROUPS)):
        wb = c_refs[g].shape[4]
        for part in range(2):
            c0 = (part + 1) * C_QKV_W // 3 + g * C_WIDTH
            new_rows = jnp.concatenate([jnp.zeros((LANES - T, C_WIDTH), F32), p[:, c0:c0 + C_WIDTH]], axis=0)
            new_t = new_rows.T
            for rb in range(C_WIDTH // LANES):
                rs = slice(rb * LANES, (rb + 1) * LANES)
                rolled = pltpu.roll(c_refs[g][0, 0, part, rs, :], wb - T, axis=1)
                if wb > LANES:
                    kv_refs[g][0, 0, part, rs, 0:wb - LANES] = rolled[:, 0:wb - LANES]
                kv_refs[g][0, 0, part, rs, wb - LANES:wb] = jnp.where(tail, new_t[rs], rolled[:, wb - LANES:wb])


def _attn_sample_call(p, caches, layer_slot, kv_prev):
    B, T, _ = p.shape
    cache_spec = lambda c: pl.BlockSpec((1, 1) + c.shape[2:], lambda b: (layer_slot, b, 0, 0, 0))
    in_specs = [pl.BlockSpec((1, T, ODD_IN_W), lambda b: (b, 0, 0))] + [cache_spec(c) for c in caches]
    args = [p] + list(caches)
    aliases = {}
    n_alias = 0
    if kv_prev is not None:
        n_alias = 3
        in_specs += [pl.BlockSpec(memory_space=pl.ANY)] * 3
        args += list(kv_prev)
        aliases = {4: 1, 5: 2, 6: 3}
    outs = pl.pallas_call(
        functools.partial(_attn_sample_kernel, n_alias),
        grid=(B,),
        in_specs=in_specs,
        out_specs=[pl.BlockSpec((1, T, C_WIDTH), lambda b: (b, 0, 0))] + [cache_spec(c) for c in caches],
        out_shape=[jax.ShapeDtypeStruct((B, T, C_WIDTH), F32)]
        + [jax.ShapeDtypeStruct(c.shape, F32) for c in caches],
        input_output_aliases=aliases,
        compiler_params=_params(("parallel",)),
        name="attn_sample",
    )(*args)
    return outs[0], outs[1:]


def _odd_out_kernel(o_ref, z_ref, x_ref, wout_ref, g_ref, b_ref, out_ref):
    y = _dot(o_ref[...] * _silu(z_ref[...]), wout_ref[...])
    out_ref[...] = _layer_norm(DEEPNORM_ALPHA * x_ref[...] + y, g_ref[...], b_ref[...])


def _odd_out_call(o, p, x, wout, g, b, tm):
    B, T, D = x.shape
    M = B * T
    out = pl.pallas_call(
        _odd_out_kernel,
        grid=(M // tm,),
        in_specs=[
            pl.BlockSpec((tm, C_WIDTH), lambda i: (i, 0)),
            pl.BlockSpec((tm, C_WIDTH), lambda i: (i, C_QKV_W // C_WIDTH)),
            pl.BlockSpec((tm, D), lambda i: (i, 0)),
            _full_spec(wout.shape), _full_spec(g.shape), _full_spec(b.shape),
        ],
        out_specs=pl.BlockSpec((tm, D), lambda i: (i, 0)),
        out_shape=jax.ShapeDtypeStruct((M, D), F32),
        compiler_params=_params(("parallel",)),
        name="odd_out",
    )(o.reshape(M, C_WIDTH), p.reshape(M, ODD_IN_W), x.reshape(M, D), wout, g, b)
    return out.reshape(B, T, D)


def kernel(x_prompt, x_sample, state_rwkv, state_shift, state_conv, cache_kv_w128, cache_kv_w512, cache_kv_w2048, even_w_in, even_w_out, rwkv_mu, rwkv_w0, rwkv_w2, rwkv_a0, rwkv_a2, rwkv_k_k, rwkv_k_a, rwkv_r_k, rwkv_gn_g, rwkv_gn_b, conv_w, odd_w_in, odd_w_out, ln_g, ln_b):
    xp, xs = x_prompt, x_sample
    BP, TP, _ = xp.shape
    BS, TS, _ = xs.shape
    row = lambda t: t.reshape(1, -1)
    to_channel_major = lambda c: jnp.transpose(c, (0, 1, 3, 4, 5, 2)).reshape(
        c.shape[0], c.shape[1], 2, C_WIDTH, c.shape[2])
    caches = [to_channel_major(c) for c in (cache_kv_w128, cache_kv_w512, cache_kv_w2048)]
    rwkv_p, rwkv_s, shift_p, shift_s, conv_p, conv_s = [], [], [], [], [], []
    kv_p = kv_s = None
    for li in range(DEPTH):
        j = li // 2
        g, b = row(ln_g[li]), row(ln_b[li])
        if li % 2 == 0:
            w_in = even_w_in[j].astype(BF16)
            w_a, w_rest = w_in[:, :A_SHIFT_W], w_in[:, A_SHIFT_W:]
            w_out = even_w_out[j].astype(BF16)
            zeros_lora = jnp.zeros((HEAD, A_WIDTH), F32)
            rw = (row(rwkv_mu[j]), row(rwkv_w0[j]), _stack_split(jnp.concatenate([rwkv_w2[j], zeros_lora], 0)),
                  row(rwkv_a0[j]), _stack_split(jnp.concatenate([zeros_lora, rwkv_a2[j]], 0)), row(rwkv_k_k[j]),
                  row(rwkv_k_a[j]), row(rwkv_r_k[j]), row(rwkv_gn_g[j]), row(rwkv_gn_b[j]))
            pa, pb = _proj_call(xp, [w_a, w_rest], tm=512)
            ya, s_new = _wkv_call(pa, jnp.zeros((BP, 1, A_SHIFT_W), F32),
                                  jnp.zeros((BP, 2 * N_PAIRS, HEAD, HEAD), F32), rw,
                                  C=WKV_CHUNK, TB=4 * WKV_CHUNK, CPI=4)
            xp, ulast = _even_out_call(ya, pb, jnp.zeros((BP, 8, B_WIDTH), F32), xp, w_out, conv_w[j], g, b,
                                       TB=4 * WKV_CHUNK)
            rwkv_p.append(s_new)
            shift_p.append(pa[:, -1, :])
            conv_p.append(ulast[:, 6:8, :])
            pa, pb = _proj_call(xs, [w_a, w_rest], tm=BS * TS)
            ya, s_new = _wkv_call(pa, state_shift[j].reshape(BS, 1, A_SHIFT_W), state_rwkv[j], rw,
                                  C=TS, TB=TS, CPI=1)
            u0 = jnp.concatenate([jnp.zeros((BS, 6, B_WIDTH), F32), state_conv[j]], axis=1)
            xs, ulast = _even_out_call(ya, pb, u0, xs, w_out, conv_w[j], g, b, TB=TS)
            rwkv_s.append(s_new)
            shift_s.append(pa[:, -1, :])
            conv_s.append(ulast[:, 6:8, :])
        else:
            w_in = odd_w_in[j].astype(BF16)
            w_out = odd_w_out[j].astype(BF16)
            pp, kv_p = _proj_odd_prompt_call(xp, w_in, j, kv_p)
            op = _attn_prompt_call(pp)
            xp = _odd_out_call(op, pp, xp, w_out, g, b, tm=512)
            (ps,) = _proj_call(xs, [w_in], tm=BS * TS)
            os_, kv_s = _attn_sample_call(ps, caches, j, kv_s)
            xs = _odd_out_call(os_, ps, xs, w_out, g, b, tm=BS * TS)
    kv_shape = lambda t: jnp.transpose(
        t.reshape(t.shape[0], t.shape[1], 2, C_WIDTH // HEAD, HEAD, t.shape[4]), (0, 1, 5, 2, 3, 4))
    return (xp, xs,
            jnp.stack(rwkv_p), jnp.stack(rwkv_s),
            jnp.stack(shift_p), jnp.stack(shift_s),
            jnp.stack(conv_p), jnp.stack(conv_s),
            kv_shape(kv_p[0]), kv_shape(kv_s[0]),
            kv_shape(kv_p[1]), kv_shape(kv_s[1]),
            kv_shape(kv_p[2]), kv_shape(kv_s[2]))
```

```python
import functools

import jax
import jax.numpy as jnp
from jax import lax
from jax.experimental import pallas as pl
from jax.experimental.pallas import tpu as pltpu

F32 = jnp.float32
BF16 = jnp.bfloat16
HIGHEST = lax.Precision.HIGHEST

D_MODEL = 1024
DEPTH = 4
HEAD = 64
LANES = 128
A_WIDTH = 512
N_PAIRS = A_WIDTH // LANES
A_SHIFT_W = 3 * A_WIDTH + 2 * HEAD
B_WIDTH = 512
EVEN_REST_W = A_WIDTH + 4 * B_WIDTH
C_GROUPS = ((128, 1), (512, 4), (2048, 16))
C_WIDTH = 512
C_QKV_W = 3 * len(C_GROUPS) * C_WIDTH
ODD_IN_W = C_QKV_W + C_WIDTH
ATTN_BLOCK = 128
GN_EPS = 64e-5
LN_EPS = 1e-5
DEEPNORM_ALPHA = (2 * DEPTH) ** 0.25
NEG_BIG = -1e30
WKV_CHUNK = 64
VMEM_LIMIT = 56 * 1024 * 1024


def _dot(a, b):
    return jnp.dot(a.astype(BF16), b.astype(BF16), preferred_element_type=F32)


def _dot_nt(a, b):
    return lax.dot_general(a.astype(BF16), b.astype(BF16), (((1,), (1,)), ((), ())),
                           preferred_element_type=F32)


def _dot_tn(a, b):
    return lax.dot_general(a.astype(BF16), b.astype(BF16), (((0,), (0,)), ((), ())),
                           preferred_element_type=F32)


def _dot_f32(a, b):
    return jnp.dot(a, b, precision=HIGHEST, preferred_element_type=F32)


def _split_bf16(x):
    hi = x.astype(BF16)
    return hi, (x - hi.astype(F32)).astype(BF16)


def _dot_split(a, b3):
    hi, lo = _split_bf16(a)
    return jnp.dot(jnp.concatenate([hi, hi, lo], axis=1), b3, preferred_element_type=F32)


def _stack_split(b):
    hi, lo = _split_bf16(b)
    return jnp.concatenate([hi, lo, hi], axis=0)


def _silu(z):
    return z * (1.0 / (1.0 + jnp.exp(-z)))


def _layer_norm(xf, g, b):
    mu = jnp.mean(xf, axis=-1, keepdims=True)
    d = xf - mu
    var = jnp.mean(d * d, axis=-1, keepdims=True)
    return d * lax.rsqrt(var + LN_EPS) * g + b


def _full_spec(shape):
    return pl.BlockSpec(shape, lambda *_: (0,) * len(shape))


def _params(semantics):
    return pltpu.CompilerParams(dimension_semantics=semantics, vmem_limit_bytes=VMEM_LIMIT)


def _proj_kernel(n_out, x_ref, *refs):
    w_refs, o_refs = refs[:n_out], refs[n_out:]
    xb = x_ref[...].astype(BF16)
    for w_ref, o_ref in zip(w_refs, o_refs):
        o_ref[...] = jnp.dot(xb, w_ref[...], preferred_element_type=F32)


def _proj_call(x, weights, tm):
    B, T, D = x.shape
    M = B * T
    outs = pl.pallas_call(
        functools.partial(_proj_kernel, len(weights)),
        grid=(M // tm,),
        in_specs=[pl.BlockSpec((tm, D), lambda i: (i, 0))] + [_full_spec(w.shape) for w in weights],
        out_specs=[pl.BlockSpec((tm, w.shape[1]), lambda i: (i, 0)) for w in weights],
        out_shape=[jax.ShapeDtypeStruct((M, w.shape[1]), F32) for w in weights],
        compiler_params=_params(("parallel",)),
        name="in_proj",
    )(x.reshape(M, D), *weights)
    return [o.reshape(B, T, -1) for o in outs]


def _kept_rows(window, T, tm):
    rows = min(window, tm)
    return rows, min(window, T) // rows


def _proj_odd_prompt_kernel(tm, T, x_ref, w_ref, *refs):
    p_ref, kv_refs = refs[-4], refs[-3:]
    i = pl.program_id(1)
    nt = T // tm
    p_ref[0] = jnp.dot(x_ref[0].astype(BF16), w_ref[...], preferred_element_type=F32)
    for g, ((window, _), kv_ref) in enumerate(zip(C_GROUPS, kv_refs)):
        rows, kt = _kept_rows(window, T, tm)

        @pl.when(i >= nt - kt)
        def _():
            for part in range(2):
                c0 = (part + 1) * C_QKV_W // 3 + g * C_WIDTH
                kv_ref[0, 0, part] = p_ref[0, tm - rows:tm, c0:c0 + C_WIDTH].T


def _proj_odd_prompt_call(x, w, layer_slot, kv_prev, tm=256):
    B, T, D = x.shape
    nt = T // tm
    n_odd = DEPTH // 2
    kv_specs, kv_shapes = [], []
    for window, _ in C_GROUPS:
        rows, kt = _kept_rows(window, T, tm)

        def index_map(b, i, kt=kt):
            return (layer_slot, b, 0, 0, jnp.maximum(i - (nt - kt), 0))

        kv_specs.append(pl.BlockSpec((1, 1, 2, C_WIDTH, rows), index_map))
        kv_shapes.append(jax.ShapeDtypeStruct((n_odd, B, 2, C_WIDTH, rows * kt), F32))
    in_specs = [pl.BlockSpec((1, tm, D), lambda b, i: (b, i, 0)), _full_spec(w.shape)]
    args = [x, w]
    aliases = {}
    if kv_prev is not None:
        in_specs += [pl.BlockSpec(memory_space=pl.ANY)] * 3
        args += list(kv_prev)
        aliases = {2: 1, 3: 2, 4: 3}
    outs = pl.pallas_call(
        functools.partial(_proj_odd_prompt_kernel, tm, T),
        grid=(B, nt),
        in_specs=in_specs,
        out_specs=[pl.BlockSpec((1, tm, ODD_IN_W), lambda b, i: (b, i, 0))] + kv_specs,
        out_shape=[jax.ShapeDtypeStruct((B, T, ODD_IN_W), F32)] + kv_shapes,
        input_output_aliases=aliases,
        compiler_params=_params(("parallel", "arbitrary")),
        name="in_proj_attn_prompt",
    )(*args)
    return outs[0], outs[1:]


def _head_sum(x, ones2):
    xh = x.astype(BF16)
    xl = (x - xh.astype(F32)).astype(BF16)
    return jnp.dot(jnp.concatenate([xh, xl], axis=1), ones2, preferred_element_type=F32)


def _prefix_sum(x, n):
    row = lax.broadcasted_iota(jnp.int32, (n, 1), 0)
    s = 1
    while s < n:
        x = x + jnp.where(row >= s, pltpu.roll(x, s, axis=0), 0.0)
        s *= 2
    return x


def _wkv_kernel(C, NCH, CPI, NSEQ, p_ref, pprev_ref, shift0_ref, s0_ref, mu_ref, w0_ref, w2p_ref, a0_ref, a2p_ref,
                kk_ref, ka_ref, rk_ref, gng_ref, gnb_ref,
                y_ref, sout_ref,
                r_s, k_s, v_s, al_s, be_s, lw_s, bo_s, y_s, st_s):
    tb = pl.program_id(1)
    TB = C * NCH
    ROWS = NSEQ * TB

    ei = lax.broadcasted_iota(jnp.int32, (HEAD, LANES), 0)
    ej = lax.broadcasted_iota(jnp.int32, (HEAD, LANES), 1)
    place = [jnp.where(ej == ei + h * HEAD, 1.0, 0.0).astype(F32) for h in range(2)]

    @pl.when(tb == 0)
    def _():
        for n in range(NSEQ):
            for hp in range(N_PAIRS):
                for h in range(2):
                    st_s[n * N_PAIRS + hp, h * HEAD:(h + 1) * HEAD, :] = _dot_f32(s0_ref[n, 2 * hp + h], place[h])

    p = jnp.concatenate([p_ref[n] for n in range(NSEQ)], axis=0)
    first_rows = [jnp.where(tb == 0, shift0_ref[n], pprev_ref[n, 7:8, :]) for n in range(NSEQ)]
    first_rows = first_rows[0] if NSEQ == 1 else jnp.concatenate(
        [jnp.broadcast_to(f, (TB, A_SHIFT_W)) for f in first_rows], axis=0)
    row = lax.broadcasted_iota(jnp.int32, (ROWS, 1), 0)
    prev = jnp.where(row % TB == 0, first_rows, pltpu.roll(p, 1, axis=0))
    ps = p + (prev - p) * mu_ref[...]
    r = ps[:, 0:A_WIDTH]
    k = ps[:, A_WIDTH:2 * A_WIDTH]
    v = ps[:, 2 * A_WIDTH:3 * A_WIDTH]
    lora_in = ps[:, 3 * A_WIDTH:A_SHIFT_W]
    nw = -(w0_ref[...] + _dot_split(jnp.tanh(lora_in), w2p_ref[...]))
    softplus = jnp.maximum(nw, 0.0) + jnp.log(1.0 + jnp.exp(-jnp.abs(nw)))
    lw = -jnp.exp(-softplus - 0.5)
    a = 1.0 / (1.0 + jnp.exp(-(a0_ref[...] + _dot_split(lora_in, a2p_ref[...]))))

    li = lax.broadcasted_iota(jnp.int32, (2 * LANES, LANES), 0)
    lj = lax.broadcasted_iota(jnp.int32, (2 * LANES, LANES), 1)
    ones2 = jnp.where(((li % LANES) // HEAD) == (lj // HEAD), 1.0, 0.0).astype(BF16)

    kk = k * kk_ref[...]
    kmod = k * (1.0 + (a - 1.0) * ka_ref[...])
    rk = r * kmod * rk_ref[...]
    sls = [slice(hp * LANES, (hp + 1) * LANES) for hp in range(N_PAIRS)]
    for sl in sls:
        kkp = kk[:, sl]
        kkn = kkp * lax.rsqrt(jnp.maximum(_head_sum(kkp * kkp, ones2), 1e-24))
        al_s[:, sl] = -kkn
        be_s[:, sl] = kkn * a[:, sl]
        bo_s[:, sl] = _head_sum(rk[:, sl], ones2) * v[:, sl]
    r_s[...] = r
    k_s[...] = kmod
    v_s[...] = v
    lw_s[...] = lw

    ti = lax.broadcasted_iota(jnp.int32, (C, C), 0)
    tj = lax.broadcasted_iota(jnp.int32, (C, C), 1)
    strict = ti > tj
    incl = ti >= tj
    eye = jnp.where(ti == tj, 1.0, 0.0).astype(F32)
    level_masks = []
    b = 1
    while b < C:
        level_masks.append(((ti // (2 * b)) == (tj // (2 * b))) & (((ti // b) % 2) == 1) & (((tj // b) % 2) == 0))
        b *= 2
    lane = lax.broadcasted_iota(jnp.int32, (1, LANES), 1)
    head_masks = (lane < HEAD, lane >= HEAD)
    m0 = head_masks[0]
    bi = lax.broadcasted_iota(jnp.int32, (LANES, LANES), 0)
    bj = lax.broadcasted_iota(jnp.int32, (LANES, LANES), 1)
    pair_diag = (bi // HEAD) == (bj // HEAD)

    def chunk_group(it, carry):
        cks = []
        for n, ci in [(n, ci) for n in range(NSEQ) for ci in range(CPI)]:
            rows = pl.ds(pl.multiple_of(n * TB + (it * CPI + ci) * C, C), C)
            lw_c = lw_s[rows, :]
            linc = _prefix_sum(lw_c, C)
            lm = linc[C // 2 - 1:C // 2, :]
            lend = linc[C - 1:C, :]
            e_fwd = jnp.exp(linc - lm)
            e_bwd = jnp.exp(lm - linc)
            cks.append(dict(
                rows=rows, seq=n,
                at=al_s[rows, :] * jnp.exp(linc - lw_c - lm),
                rt=r_s[rows, :] * e_fwd,
                bt=be_s[rows, :] * e_bwd,
                kt=k_s[rows, :] * e_bwd,
                vv=v_s[rows, :],
                em=jnp.exp(lm), g=jnp.exp(lend - lm), pc=jnp.exp(lend)))
        units = [(ck, hp, h) for ck in cks for hp in range(N_PAIRS) for h in range(2)]
        pairs = [(ck, hp) for ck in cks for hp in range(N_PAIRS)]
        gb, gk = [], []
        for ck, hp, h in units:
            x = jnp.concatenate([jnp.where(head_masks[h], ck["at"][:, sls[hp]], 0.0),
                                 jnp.where(head_masks[h], ck["rt"][:, sls[hp]], 0.0)], axis=0)
            gb.append(_dot_nt(x, ck["bt"][:, sls[hp]]))
            gk.append(_dot_nt(x, ck["kt"][:, sls[hp]]))
        aab = [jnp.where(strict, m[:C], 0.0) for m in gb]
        aak = [jnp.where(strict, m[:C], 0.0) for m in gk]
        arb = [jnp.where(incl, m[C:], 0.0) for m in gb]
        ark = [jnp.where(incl, m[C:], 0.0) for m in gk]
        tinv = [eye + jnp.where(level_masks[0], n, 0.0) for n in aab]
        for lmask in level_masks[1:]:
            tmp = [_dot(t, jnp.where(lmask, n, 0.0)) for t, n in zip(tinv, aab)]
            tinv = [t + _dot(x, t) for t, x in zip(tinv, tmp)]
        av = [_dot(aak[i], ck["vv"][:, sls[hp]]) for i, (ck, hp, h) in enumerate(units)]
        zz = [jnp.concatenate([ck["at"][:, sls[hp]], jnp.where(m0, av[2 * j], av[2 * j + 1])], axis=1)
              for j, (ck, hp) in enumerate(pairs)]
        tz = [_dot(tinv[i], zz[i // 2]) for i in range(len(units))]
        ahat = [jnp.where(m0, tz[2 * j][:, :LANES], tz[2 * j + 1][:, :LANES]) for j in range(len(pairs))]
        wmat = [jnp.where(m0, tz[2 * j][:, LANES:], tz[2 * j + 1][:, LANES:]) for j in range(len(pairs))]
        aw = [jnp.concatenate([ahat[j], wmat[j]], axis=1) for j in range(len(pairs))]
        ry = [_dot(arb[i], aw[i // 2]) for i in range(len(units))]
        akv = [_dot(ark[i], ck["vv"][:, sls[hp]]) for i, (ck, hp, h) in enumerate(units)]
        rhat_e, yi, mp, np_ = [], [], [], []
        for j, (ck, hp) in enumerate(pairs):
            sl = sls[hp]
            em_p, g_p = ck["em"][:, sl], ck["g"][:, sl]
            bg = ck["bt"][:, sl] * g_p
            kg = ck["kt"][:, sl] * g_p
            rhat_e.append((ck["rt"][:, sl] + jnp.where(m0, ry[2 * j][:, :LANES], ry[2 * j + 1][:, :LANES])) * em_p)
            yi.append(jnp.where(m0, ry[2 * j][:, LANES:] + akv[2 * j], ry[2 * j + 1][:, LANES:] + akv[2 * j + 1]))
            mp.append(jnp.where(pair_diag, _dot_tn(ahat[j] * em_p, bg), 0.0))
            np_.append(jnp.where(pair_diag, _dot_tn(jnp.concatenate([wmat[j], ck["vv"][:, sl]], axis=0),
                                                    jnp.concatenate([bg, kg], axis=0)), 0.0))
        for n in range(NSEQ):
            for hp in range(N_PAIRS):
                s_bd = st_s[n * N_PAIRS + hp]
                for ci, ck in enumerate(cks):
                    if ck["seq"] != n:
                        continue
                    j = ci * N_PAIRS + hp
                    y_s[ck["rows"], sls[hp]] = _dot_nt(rhat_e[j], s_bd) + yi[j]
                    s_bd = s_bd * ck["pc"][:, sls[hp]] + _dot(s_bd, mp[j]) + np_[j]
                st_s[n * N_PAIRS + hp] = s_bd
        return carry

    lax.fori_loop(0, NCH // CPI, chunk_group, 0)

    for sl in sls:
        y = y_s[:, sl]
        mean = _head_sum(y, ones2) * (1.0 / HEAD)
        d = y - mean
        var = _head_sum(d * d, ones2) * (1.0 / HEAD)
        out = d * lax.rsqrt(var + GN_EPS) * gng_ref[:, sl] + gnb_ref[:, sl] + bo_s[:, sl]
        for n in range(NSEQ):
            y_ref[n, :, sl] = out[n * TB:(n + 1) * TB]

    @pl.when(tb == pl.num_programs(1) - 1)
    def _():
        for n in range(NSEQ):
            for hp in range(N_PAIRS):
                for h in range(2):
                    sout_ref[n, 2 * hp + h] = lax.dot_general(
                        st_s[n * N_PAIRS + hp, h * HEAD:(h + 1) * HEAD, :], place[h], (((1,), (1,)), ((), ())),
                        precision=HIGHEST, preferred_element_type=F32)


def _wkv_call(p, shift0, s0, rw, *, C, TB, CPI, NSEQ=1):
    B, T, _ = p.shape
    H = 2 * N_PAIRS
    assert NSEQ == 1 or (T == TB and TB // C == CPI)
    return pl.pallas_call(
        functools.partial(_wkv_kernel, C, TB // C, CPI, NSEQ),
        grid=(B // NSEQ, T // TB),
        in_specs=[
            pl.BlockSpec((NSEQ, TB, A_SHIFT_W), lambda b, t: (b, t, 0)),
            pl.BlockSpec((NSEQ, 8, A_SHIFT_W), lambda b, t: (b, jnp.maximum(t * (TB // 8) - 1, 0), 0)),
            pl.BlockSpec((NSEQ, 1, A_SHIFT_W), lambda b, t: (b, 0, 0)),
            pl.BlockSpec((NSEQ, H, HEAD, HEAD), lambda b, t: (b, 0, 0, 0)),
        ] + [_full_spec(x.shape) for x in rw],
        out_specs=[
            pl.BlockSpec((NSEQ, TB, A_WIDTH), lambda b, t: (b, t, 0)),
            pl.BlockSpec((NSEQ, H, HEAD, HEAD), lambda b, t: (b, 0, 0, 0)),
        ],
        out_shape=[jax.ShapeDtypeStruct((B, T, A_WIDTH), F32),
                   jax.ShapeDtypeStruct((B, H, HEAD, HEAD), F32)],
        scratch_shapes=[pltpu.VMEM((NSEQ * TB, A_WIDTH), F32) for _ in range(8)]
        + [pltpu.VMEM((NSEQ * N_PAIRS, LANES, LANES), F32)],
        compiler_params=_params(("parallel", "arbitrary")),
        name="wkv7_chunked",
    )(p, p, shift0, s0, *rw)


def _even_out_kernel(ya_ref, pb_ref, pbprev_ref, u0_ref, x_ref, wout_ref, cw_ref, g_ref, b_ref,
                     o_ref, ulast_ref):
    tb = pl.program_id(1)
    TB = ya_ref.shape[1]
    pb = pb_ref[0]
    za, bb, cc, hh, zb = (pb[:, i * 512:(i + 1) * 512] for i in range(5))
    u = cc * hh
    pprev = pbprev_ref[0]
    uprev = jnp.where(tb == 0, u0_ref[0], pprev[:, 1024:1536] * pprev[:, 1536:2048])
    row = lax.broadcasted_iota(jnp.int32, (TB, 1), 0)
    u1 = jnp.where(row == 0, uprev[7:8], pltpu.roll(u, 1, axis=0))
    u2 = jnp.where(row == 0, uprev[6:7], jnp.where(row == 1, uprev[7:8], pltpu.roll(u, 2, axis=0)))
    cw = cw_ref[...]
    yb = bb * (cw[0:1] * u2 + cw[1:2] * u1 + cw[2:3] * u)
    mix_a = ya_ref[0] * _silu(za)
    mix_b = yb * _silu(zb)
    y = _dot(mix_a, wout_ref[0:A_WIDTH, :]) + _dot(mix_b, wout_ref[A_WIDTH:, :])
    o_ref[0] = _layer_norm(DEEPNORM_ALPHA * x_ref[0] + y, g_ref[...], b_ref[...])
    ulast_ref[0] = u[TB - 8:TB]


def _even_out_call(ya, pb, u0, x, wout, cw, g, b, TB):
    B, T, _ = x.shape
    return pl.pallas_call(
        _even_out_kernel,
        grid=(B, T // TB),
        in_specs=[
            pl.BlockSpec((1, TB, A_WIDTH), lambda b_, t: (b_, t, 0)),
            pl.BlockSpec((1, TB, EVEN_REST_W), lambda b_, t: (b_, t, 0)),
            pl.BlockSpec((1, 8, EVEN_REST_W), lambda b_, t: (b_, jnp.maximum(t * (TB // 8) - 1, 0), 0)),
            pl.BlockSpec((1, 8, B_WIDTH), lambda b_, t: (b_, 0, 0)),
            pl.BlockSpec((1, TB, D_MODEL), lambda b_, t: (b_, t, 0)),
            _full_spec(wout.shape), _full_spec(cw.shape), _full_spec(g.shape), _full_spec(b.shape),
        ],
        out_specs=[pl.BlockSpec((1, TB, D_MODEL), lambda b_, t: (b_, t, 0)),
                   pl.BlockSpec((1, 8, B_WIDTH), lambda b_, t: (b_, 0, 0))],
        out_shape=[jax.ShapeDtypeStruct((B, T, D_MODEL), F32), jax.ShapeDtypeStruct((B, 8, B_WIDTH), F32)],
        compiler_params=_params(("parallel", "arbitrary")),
        name="even_out",
    )(ya, pb, pb, u0, x, wout, cw, g, b)


def _attn_prompt_kernel(*refs):
    q_refs, k_refs, v_refs = refs[0:3], refs[3:6], refs[6:9]
    o_ref, og_s, lse_s = refs[9], refs[10], refs[11]
    BLK = ATTN_BLOCK
    lane = lax.broadcasted_iota(jnp.int32, (1, LANES), 1)
    m0 = lane < HEAD
    qi = lax.broadcasted_iota(jnp.int32, (2 * BLK, 2 * BLK), 0) % BLK
    kj = lax.broadcasted_iota(jnp.int32, (2 * BLK, 2 * BLK), 1)
    band = (kj >= qi) & (kj <= qi + BLK)
    qi1 = lax.broadcasted_iota(jnp.int32, (2 * BLK, BLK), 0) % BLK
    kj1 = lax.broadcasted_iota(jnp.int32, (2 * BLK, BLK), 1)
    causal = kj1 <= qi1
    for g, (window, d) in enumerate(C_GROUPS):
        assert window // d == BLK
        nb = q_refs[g].shape[1] // (d * BLK)
        for rho in range(d):
            k_prev = v_prev = None
            for i in range(nb):
                start = rho + d * BLK * i
                rows = pl.ds(start, BLK, stride=d) if d > 1 else pl.ds(start, BLK)
                q = q_refs[g][0, rows, :] * (HEAD ** -0.5)
                k_cur = k_refs[g][0, rows, :].astype(BF16)
                v_cur = v_refs[g][0, rows, :].astype(BF16)
                if i == 0:
                    keys, vals, mask = k_cur, v_cur, causal
                else:
                    keys = jnp.concatenate([k_prev, k_cur], axis=0)
                    vals = jnp.concatenate([v_prev, v_cur], axis=0)
                    mask = band
                q2 = jnp.concatenate([jnp.where(m0, q, 0.0), jnp.where(m0, 0.0, q)], axis=0)
                s = jnp.where(mask, _dot_nt(q2, keys), NEG_BIG)
                mx = jnp.max(s, axis=-1, keepdims=True)
                e = jnp.exp(s - mx)
                den = jnp.sum(e, axis=-1, keepdims=True)
                o2 = _dot(e, vals) * (1.0 / den)
                lse2 = mx + jnp.log(den)
                og_s[g, rows, :] = jnp.where(m0, o2[:BLK], o2[BLK:])
                lse_s[g, rows, :] = jnp.where(m0, lse2[:BLK], lse2[BLK:])
                k_prev, v_prev = k_cur, v_cur
    l0, l1, l2 = lse_s[0], lse_s[1], lse_s[2]
    mx = jnp.maximum(jnp.maximum(l0, l1), l2)
    w0, w1, w2 = jnp.exp(l0 - mx), jnp.exp(l1 - mx), jnp.exp(l2 - mx)
    o_ref[0] = (w0 * og_s[0] + w1 * og_s[1] + w2 * og_s[2]) * (1.0 / (w0 + w1 + w2))


def _attn_prompt_call(p):
    B, T, _ = p.shape
    n_g = len(C_GROUPS)

    def col_spec(part, g):
        base = (part * n_g + g) * C_WIDTH // LANES
        return pl.BlockSpec((1, T, LANES), lambda b, hp: (b, 0, base + hp))

    specs = [col_spec(part, g) for part in range(3) for g in range(n_g)]
    return pl.pallas_call(
        _attn_prompt_kernel,
        grid=(B, N_PAIRS),
        in_specs=specs,
        out_specs=pl.BlockSpec((1, T, LANES), lambda b, hp: (b, 0, hp)),
        out_shape=jax.ShapeDtypeStruct((B, T, C_WIDTH), F32),
        scratch_shapes=[pltpu.VMEM((n_g, T, LANES), F32), pltpu.VMEM((n_g, T, LANES), F32)],
        compiler_params=_params(("parallel", "parallel")),
        name="attn_prompt",
    )(*([p] * 9))


def _attn_sample_kernel(n_alias, ps_ref, *refs):
    c_refs = refs[0:3]
    o_ref = refs[3 + n_alias]
    kv_refs = refs[4 + n_alias:7 + n_alias]
    T = ps_ref.shape[1]
    p = ps_ref[0]
    lane = lax.broadcasted_iota(jnp.int32, (1, LANES), 1)
    m0 = lane < HEAD
    tq = lax.broadcasted_iota(jnp.int32, (2 * T, 1), 0) % T
    for hp in range(N_PAIRS):
        m_run = jnp.full((2 * T, 1), NEG_BIG, F32)
        l_run = jnp.zeros((2 * T, 1), F32)
        acc = jnp.zeros((2 * T, LANES), F32)
        ch = slice(hp * LANES, (hp + 1) * LANES)
        for g, (window, d) in enumerate(C_GROUPS):
            wb = c_refs[g].shape[4]
            assert wb == window and window % d == 0
            col = g * C_WIDTH + hp * LANES
            q = p[:, col:col + LANES] * (HEAD ** -0.5)
            k_new = p[:, C_QKV_W // 3 + col:C_QKV_W // 3 + col + LANES]
            v_new = p[:, 2 * C_QKV_W // 3 + col:2 * C_QKV_W // 3 + col + LANES]
            q2 = jnp.concatenate([jnp.where(m0, q, 0.0), jnp.where(m0, 0.0, q)], axis=0)
            pos = lax.broadcasted_iota(jnp.int32, (2 * T, wb), 1)
            ok_old = (pos >= tq + (wb - window)) & (((pos - tq) & (d - 1)) == 0)
            tn = lax.broadcasted_iota(jnp.int32, (2 * T, T), 1)
            ok_new = (tn <= tq) & (((tq - tn) & (d - 1)) == 0)
            s_old = jnp.where(ok_old, _dot(q2, c_refs[g][0, 0, 0, ch, :]), NEG_BIG)
            s_new = jnp.where(ok_new, _dot_nt(q2, k_new), NEG_BIG)
            m_new = jnp.maximum(m_run, jnp.maximum(jnp.max(s_old, axis=-1, keepdims=True),
                                                   jnp.max(s_new, axis=-1, keepdims=True)))
            e_old = jnp.exp(s_old - m_new)
            e_new = jnp.exp(s_new - m_new)
            scale = jnp.exp(m_run - m_new)
            l_run = l_run * scale + jnp.sum(e_old, axis=-1, keepdims=True) + jnp.sum(e_new, axis=-1, keepdims=True)
            acc = acc * scale + _dot_nt(e_old, c_refs[g][0, 0, 1, ch, :]) + _dot(e_new, v_new)
            m_run = m_new
        o2 = acc * (1.0 / l_run)
        o_ref[0, :, ch] = jnp.where(m0, o2[:T], o2[T:])
    tail = lane >= LANES - T
    for g in range(len(C_GROUPS)):
        wb = c_refs[g].shape[4]
        for part in range(2):
            c0 = (part + 1) * C_QKV_W // 3 + g * C_WIDTH
            new_rows = jnp.concatenate([jnp.zeros((LANES - T, C_WIDTH), F32), p[:, c0:c0 + C_WIDTH]], axis=0)
            new_t = new_rows.T
            for rb in range(C_WIDTH // LANES):
                rs = slice(rb * LANES, (rb + 1) * LANES)
                rolled = pltpu.roll(c_refs[g][0, 0, part, rs, :], wb - T, axis=1)
                if wb > LANES:
                    kv_refs[g][0, 0, part, rs, 0:wb - LANES] = rolled[:, 0:wb - LANES]
                kv_refs[g][0, 0, part, rs, wb - LANES:wb] = jnp.where(tail, new_t[rs], rolled[:, wb - LANES:wb])


def _attn_sample_call(p, caches, layer_slot, kv_prev):
    B, T, _ = p.shape
    cache_spec = lambda c: pl.BlockSpec((1, 1) + c.shape[2:], lambda b: (layer_slot, b, 0, 0, 0))
    in_specs = [pl.BlockSpec((1, T, ODD_IN_W), lambda b: (b, 0, 0))] + [cache_spec(c) for c in caches]
    args = [p] + list(caches)
    aliases = {}
    n_alias = 0
    if kv_prev is not None:
        n_alias = 3
        in_specs += [pl.BlockSpec(memory_space=pl.ANY)] * 3
        args += list(kv_prev)
        aliases = {4: 1, 5: 2, 6: 3}
    outs = pl.pallas_call(
        functools.partial(_attn_sample_kernel, n_alias),
        grid=(B,),
        in_specs=in_specs,
        out_specs=[pl.BlockSpec((1, T, C_WIDTH), lambda b: (b, 0, 0))] + [cache_spec(c) for c in caches],
        out_shape=[jax.ShapeDtypeStruct((B, T, C_WIDTH), F32)]
        + [jax.ShapeDtypeStruct(c.shape, F32) for c in caches],
        input_output_aliases=aliases,
        compiler_params=_params(("parallel",)),
        name="attn_sample",
    )(*args)
    return outs[0], outs[1:]


def _odd_out_kernel(o_ref, z_ref, x_ref, wout_ref, g_ref, b_ref, out_ref):
    y = _dot(o_ref[...] * _silu(z_ref[...]), wout_ref[...])
    out_ref[...] = _layer_norm(DEEPNORM_ALPHA * x_ref[...] + y, g_ref[...], b_ref[...])


def _odd_out_call(o, p, x, wout, g, b, tm):
    B, T, D = x.shape
    M = B * T
    out = pl.pallas_call(
        _odd_out_kernel,
        grid=(M // tm,),
        in_specs=[
            pl.BlockSpec((tm, C_WIDTH), lambda i: (i, 0)),
            pl.BlockSpec((tm, C_WIDTH), lambda i: (i, C_QKV_W // C_WIDTH)),
            pl.BlockSpec((tm, D), lambda i: (i, 0)),
            _full_spec(wout.shape), _full_spec(g.shape), _full_spec(b.shape),
        ],
        out_specs=pl.BlockSpec((tm, D), lambda i: (i, 0)),
        out_shape=jax.ShapeDtypeStruct((M, D), F32),
        compiler_params=_params(("parallel",)),
        name="odd_out",
    )(o.reshape(M, C_WIDTH), p.reshape(M, ODD_IN_W), x.reshape(M, D), wout, g, b)
    return out.reshape(B, T, D)


def kernel(x_prompt, x_sample, state_rwkv, state_shift, state_conv, cache_kv_w128, cache_kv_w512, cache_kv_w2048, even_w_in, even_w_out, rwkv_mu, rwkv_w0, rwkv_w2, rwkv_a0, rwkv_a2, rwkv_k_k, rwkv_k_a, rwkv_r_k, rwkv_gn_g, rwkv_gn_b, conv_w, odd_w_in, odd_w_out, ln_g, ln_b):
    xp, xs = x_prompt, x_sample
    BP, TP, _ = xp.shape
    BS, TS, _ = xs.shape
    row = lambda t: t.reshape(1, -1)
    to_channel_major = lambda c: jnp.transpose(c, (0, 1, 3, 4, 5, 2)).reshape(
        c.shape[0], c.shape[1], 2, C_WIDTH, c.shape[2])
    caches = [to_channel_major(c) for c in (cache_kv_w128, cache_kv_w512, cache_kv_w2048)]
    rwkv_p, rwkv_s, shift_p, shift_s, conv_p, conv_s = [], [], [], [], [], []
    kv_p = kv_s = None
    for li in range(DEPTH):
        j = li // 2
        g, b = row(ln_g[li]), row(ln_b[li])
        if li % 2 == 0:
            w_in = even_w_in[j].astype(BF16)
            w_a, w_rest = w_in[:, :A_SHIFT_W], w_in[:, A_SHIFT_W:]
            w_out = even_w_out[j].astype(BF16)
            zeros_lora = jnp.zeros((HEAD, A_WIDTH), F32)
            rw = (row(rwkv_mu[j]), row(rwkv_w0[j]), _stack_split(jnp.concatenate([rwkv_w2[j], zeros_lora], 0)),
                  row(rwkv_a0[j]), _stack_split(jnp.concatenate([zeros_lora, rwkv_a2[j]], 0)), row(rwkv_k_k[j]),
                  row(rwkv_k_a[j]), row(rwkv_r_k[j]), row(rwkv_gn_g[j]), row(rwkv_gn_b[j]))
            pa, pb = _proj_call(xp, [w_a, w_rest], tm=512)
            ya, s_new = _wkv_call(pa, jnp.zeros((BP, 1, A_SHIFT_W), F32),
                                  jnp.zeros((BP, 2 * N_PAIRS, HEAD, HEAD), F32), rw,
                                  C=WKV_CHUNK, TB=8 * WKV_CHUNK, CPI=4)
            xp, ulast = _even_out_call(ya, pb, jnp.zeros((BP, 8, B_WIDTH), F32), xp, w_out, conv_w[j], g, b,
                                       TB=4 * WKV_CHUNK)
            rwkv_p.append(s_new)
            shift_p.append(pa[:, -1, :])
            conv_p.append(ulast[:, 6:8, :])
            pa, pb = _proj_call(xs, [w_a, w_rest], tm=BS * TS)
            ya, s_new = _wkv_call(pa, state_shift[j].reshape(BS, 1, A_SHIFT_W), state_rwkv[j], rw,
                                  C=TS, TB=TS, CPI=1, NSEQ=4)
            u0 = jnp.concatenate([jnp.zeros((BS, 6, B_WIDTH), F32), state_conv[j]], axis=1)
            xs, ulast = _even_out_call(ya, pb, u0, xs, w_out, conv_w[j], g, b, TB=TS)
            rwkv_s.append(s_new)
            shift_s.append(pa[:, -1, :])
            conv_s.append(ulast[:, 6:8, :])
        else:
            w_in = odd_w_in[j].astype(BF16)
            w_out = odd_w_out[j].astype(BF16)
            pp, kv_p = _proj_odd_prompt_call(xp, w_in, j, kv_p)
            op = _attn_prompt_call(pp)
            xp = _odd_out_call(op, pp, xp, w_out, g, b, tm=512)
            (ps,) = _proj_call(xs, [w_in], tm=BS * TS)
            os_, kv_s = _attn_sample_call(ps, caches, j, kv_s)
            xs = _odd_out_call(os_, ps, xs, w_out, g, b, tm=BS * TS)
    kv_shape = lambda t: jnp.transpose(
        t.reshape(t.shape[0], t.shape[1], 2, C_WIDTH // HEAD, HEAD, t.shape[4]), (0, 1, 5, 2, 3, 4))
    return (xp, xs,
            jnp.stack(rwkv_p), jnp.stack(rwkv_s),
            jnp.stack(shift_p), jnp.stack(shift_s),
            jnp.stack(conv_p), jnp.stack(conv_s),
            kv_shape(kv_p[0]), kv_shape(kv_s[0]),
            kv_shape(kv_p[1]), kv_shape(kv_s[1]),
            kv_shape(kv_p[2]), kv_shape(kv_s[2]))
```

```python
import functools

import jax
import jax.numpy as jnp
from jax import lax
from jax.experimental import pallas as pl
from jax.experimental.pallas import tpu as pltpu

F32 = jnp.float32
BF16 = jnp.bfloat16
HIGHEST = lax.Precision.HIGHEST

D_MODEL = 1024
DEPTH = 4
HEAD = 64
LANES = 128
A_WIDTH = 512
N_PAIRS = A_WIDTH // LANES
A_SHIFT_W = 3 * A_WIDTH + 2 * HEAD
B_WIDTH = 512
EVEN_REST_W = A_WIDTH + 4 * B_WIDTH
C_GROUPS = ((128, 1), (512, 4), (2048, 16))
C_WIDTH = 512
C_QKV_W = 3 * len(C_GROUPS) * C_WIDTH
ODD_IN_W = C_QKV_W + C_WIDTH
ATTN_BLOCK = 128
GN_EPS = 64e-5
LN_EPS = 1e-5
DEEPNORM_ALPHA = (2 * DEPTH) ** 0.25
NEG_BIG = -1e30
WKV_CHUNK = 64
VMEM_LIMIT = 56 * 1024 * 1024


def _dot(a, b):
    return jnp.dot(a.astype(BF16), b.astype(BF16), preferred_element_type=F32)


def _dot_nt(a, b):
    return lax.dot_general(a.astype(BF16), b.astype(BF16), (((1,), (1,)), ((), ())),
                           preferred_element_type=F32)


def _dot_tn(a, b):
    return lax.dot_general(a.astype(BF16), b.astype(BF16), (((0,), (0,)), ((), ())),
                           preferred_element_type=F32)


def _dot_f32(a, b):
    return jnp.dot(a, b, precision=HIGHEST, preferred_element_type=F32)


def _split_bf16(x):
    hi = x.astype(BF16)
    return hi, (x - hi.astype(F32)).astype(BF16)


def _dot_split(a, b3):
    hi, lo = _split_bf16(a)
    return jnp.dot(jnp.concatenate([hi, hi, lo], axis=1), b3, preferred_element_type=F32)


def _stack_split(b):
    hi, lo = _split_bf16(b)
    return jnp.concatenate([hi, lo, hi], axis=0)


def _silu(z):
    return z * (1.0 / (1.0 + jnp.exp(-z)))


def _layer_norm(xf, g, b):
    mu = jnp.mean(xf, axis=-1, keepdims=True)
    d = xf - mu
    var = jnp.mean(d * d, axis=-1, keepdims=True)
    return d * lax.rsqrt(var + LN_EPS) * g + b


def _full_spec(shape):
    return pl.BlockSpec(shape, lambda *_: (0,) * len(shape))


def _params(semantics):
    return pltpu.CompilerParams(dimension_semantics=semantics, vmem_limit_bytes=VMEM_LIMIT)


def _proj_kernel(n_out, x_ref, *refs):
    w_refs, o_refs = refs[:n_out], refs[n_out:]
    xb = x_ref[...].astype(BF16)
    for w_ref, o_ref in zip(w_refs, o_refs):
        o_ref[...] = jnp.dot(xb, w_ref[...], preferred_element_type=F32).astype(o_ref.dtype)


def _proj_call(x, weights, tm, out_dtypes=None):
    B, T, D = x.shape
    M = B * T
    out_dtypes = out_dtypes or [F32] * len(weights)
    outs = pl.pallas_call(
        functools.partial(_proj_kernel, len(weights)),
        grid=(M // tm,),
        in_specs=[pl.BlockSpec((tm, D), lambda i: (i, 0))] + [_full_spec(w.shape) for w in weights],
        out_specs=[pl.BlockSpec((tm, w.shape[1]), lambda i: (i, 0)) for w in weights],
        out_shape=[jax.ShapeDtypeStruct((M, w.shape[1]), dt) for w, dt in zip(weights, out_dtypes)],
        compiler_params=_params(("parallel",)),
        name="in_proj",
    )(x.reshape(M, D), *weights)
    return [o.reshape(B, T, -1) for o in outs]


def _kept_rows(window, T, tm):
    rows = min(window, tm)
    return rows, min(window, T) // rows


def _proj_odd_prompt_kernel(tm, T, x_ref, w_ref, *refs):
    p_ref, kv_refs = refs[-4], refs[-3:]
    i = pl.program_id(1)
    nt = T // tm
    p_ref[0] = jnp.dot(x_ref[0].astype(BF16), w_ref[...], preferred_element_type=F32)
    for g, ((window, _), kv_ref) in enumerate(zip(C_GROUPS, kv_refs)):
        rows, kt = _kept_rows(window, T, tm)

        @pl.when(i >= nt - kt)
        def _():
            for part in range(2):
                c0 = (part + 1) * C_QKV_W // 3 + g * C_WIDTH
                kv_ref[0, 0, part] = p_ref[0, tm - rows:tm, c0:c0 + C_WIDTH].T


def _proj_odd_prompt_call(x, w, layer_slot, kv_prev, tm=256):
    B, T, D = x.shape
    nt = T // tm
    n_odd = DEPTH // 2
    kv_specs, kv_shapes = [], []
    for window, _ in C_GROUPS:
        rows, kt = _kept_rows(window, T, tm)

        def index_map(b, i, kt=kt):
            return (layer_slot, b, 0, 0, jnp.maximum(i - (nt - kt), 0))

        kv_specs.append(pl.BlockSpec((1, 1, 2, C_WIDTH, rows), index_map))
        kv_shapes.append(jax.ShapeDtypeStruct((n_odd, B, 2, C_WIDTH, rows * kt), F32))
    in_specs = [pl.BlockSpec((1, tm, D), lambda b, i: (b, i, 0)), _full_spec(w.shape)]
    args = [x, w]
    aliases = {}
    if kv_prev is not None:
        in_specs += [pl.BlockSpec(memory_space=pl.ANY)] * 3
        args += list(kv_prev)
        aliases = {2: 1, 3: 2, 4: 3}
    outs = pl.pallas_call(
        functools.partial(_proj_odd_prompt_kernel, tm, T),
        grid=(B, nt),
        in_specs=in_specs,
        out_specs=[pl.BlockSpec((1, tm, ODD_IN_W), lambda b, i: (b, i, 0))] + kv_specs,
        out_shape=[jax.ShapeDtypeStruct((B, T, ODD_IN_W), F32)] + kv_shapes,
        input_output_aliases=aliases,
        compiler_params=_params(("parallel", "arbitrary")),
        name="in_proj_attn_prompt",
    )(*args)
    return outs[0], outs[1:]


def _head_sum(x, ones2):
    xh = x.astype(BF16)
    xl = (x - xh.astype(F32)).astype(BF16)
    return jnp.dot(jnp.concatenate([xh, xl], axis=1), ones2, preferred_element_type=F32)


def _prefix_sum(x, n):
    row = lax.broadcasted_iota(jnp.int32, (n, 1), 0)
    s = 1
    while s < n:
        x = x + jnp.where(row >= s, pltpu.roll(x, s, axis=0), 0.0)
        s *= 2
    return x


def _wkv_kernel(C, NCH, CPI, NSEQ, p_ref, pprev_ref, shift0_ref, s0_ref, mu_ref, w0_ref, w2p_ref, a0_ref, a2p_ref,
                kk_ref, ka_ref, rk_ref, gng_ref, gnb_ref,
                y_ref, sout_ref,
                r_s, k_s, v_s, al_s, be_s, lw_s, bo_s, y_s, st_s):
    tb = pl.program_id(1)
    TB = C * NCH
    ROWS = NSEQ * TB

    ei = lax.broadcasted_iota(jnp.int32, (HEAD, LANES), 0)
    ej = lax.broadcasted_iota(jnp.int32, (HEAD, LANES), 1)
    place = [jnp.where(ej == ei + h * HEAD, 1.0, 0.0).astype(F32) for h in range(2)]

    @pl.when(tb == 0)
    def _():
        for n in range(NSEQ):
            for hp in range(N_PAIRS):
                for h in range(2):
                    st_s[n * N_PAIRS + hp, h * HEAD:(h + 1) * HEAD, :] = _dot_f32(s0_ref[n, 2 * hp + h], place[h])

    p = jnp.concatenate([p_ref[n] for n in range(NSEQ)], axis=0)
    first_rows = [jnp.where(tb == 0, shift0_ref[n], pprev_ref[n, 7:8, :]) for n in range(NSEQ)]
    first_rows = first_rows[0] if NSEQ == 1 else jnp.concatenate(
        [jnp.broadcast_to(f, (TB, A_SHIFT_W)) for f in first_rows], axis=0)
    row = lax.broadcasted_iota(jnp.int32, (ROWS, 1), 0)
    prev = jnp.where(row % TB == 0, first_rows, pltpu.roll(p, 1, axis=0))
    ps = p + (prev - p) * mu_ref[...]
    r = ps[:, 0:A_WIDTH]
    k = ps[:, A_WIDTH:2 * A_WIDTH]
    v = ps[:, 2 * A_WIDTH:3 * A_WIDTH]
    lora_in = ps[:, 3 * A_WIDTH:A_SHIFT_W]
    nw = -(w0_ref[...] + _dot_split(jnp.tanh(lora_in), w2p_ref[...]))
    softplus = jnp.maximum(nw, 0.0) + jnp.log(1.0 + jnp.exp(-jnp.abs(nw)))
    lw = -jnp.exp(-softplus - 0.5)
    a = 1.0 / (1.0 + jnp.exp(-(a0_ref[...] + _dot_split(lora_in, a2p_ref[...]))))

    li = lax.broadcasted_iota(jnp.int32, (2 * LANES, LANES), 0)
    lj = lax.broadcasted_iota(jnp.int32, (2 * LANES, LANES), 1)
    ones2 = jnp.where(((li % LANES) // HEAD) == (lj // HEAD), 1.0, 0.0).astype(BF16)

    kk = k * kk_ref[...]
    kmod = k * (1.0 + (a - 1.0) * ka_ref[...])
    rk = r * kmod * rk_ref[...]
    sls = [slice(hp * LANES, (hp + 1) * LANES) for hp in range(N_PAIRS)]
    for sl in sls:
        kkp = kk[:, sl]
        kkn = kkp * lax.rsqrt(jnp.maximum(_head_sum(kkp * kkp, ones2), 1e-24))
        al_s[:, sl] = -kkn
        be_s[:, sl] = kkn * a[:, sl]
        bo_s[:, sl] = _head_sum(rk[:, sl], ones2) * v[:, sl]
    r_s[...] = r
    k_s[...] = kmod
    v_s[...] = v
    lw_s[...] = lw

    ti = lax.broadcasted_iota(jnp.int32, (C, C), 0)
    tj = lax.broadcasted_iota(jnp.int32, (C, C), 1)
    strict = ti > tj
    incl = ti >= tj
    eye = jnp.where(ti == tj, 1.0, 0.0).astype(F32)
    level_masks = []
    b = 1
    while b < C:
        level_masks.append(((ti // (2 * b)) == (tj // (2 * b))) & (((ti // b) % 2) == 1) & (((tj // b) % 2) == 0))
        b *= 2
    lane = lax.broadcasted_iota(jnp.int32, (1, LANES), 1)
    head_masks = (lane < HEAD, lane >= HEAD)
    m0 = head_masks[0]
    bi = lax.broadcasted_iota(jnp.int32, (LANES, LANES), 0)
    bj = lax.broadcasted_iota(jnp.int32, (LANES, LANES), 1)
    pair_diag = (bi // HEAD) == (bj // HEAD)

    def chunk_group(it, carry):
        cks = []
        for n, ci in [(n, ci) for n in range(NSEQ) for ci in range(CPI)]:
            rows = pl.ds(pl.multiple_of(n * TB + (it * CPI + ci) * C, C), C)
            lw_c = lw_s[rows, :]
            linc = _prefix_sum(lw_c, C)
            lm = linc[C // 2 - 1:C // 2, :]
            lend = linc[C - 1:C, :]
            e_fwd = jnp.exp(linc - lm)
            e_bwd = jnp.exp(lm - linc)
            cks.append(dict(
                rows=rows, seq=n,
                at=al_s[rows, :] * jnp.exp(linc - lw_c - lm),
                rt=r_s[rows, :] * e_fwd,
                bt=be_s[rows, :] * e_bwd,
                kt=k_s[rows, :] * e_bwd,
                vv=v_s[rows, :],
                em=jnp.exp(lm), g=jnp.exp(lend - lm), pc=jnp.exp(lend)))
        units = [(ck, hp, h) for ck in cks for hp in range(N_PAIRS) for h in range(2)]
        pairs = [(ck, hp) for ck in cks for hp in range(N_PAIRS)]
        gb, gk = [], []
        for ck, hp, h in units:
            x = jnp.concatenate([jnp.where(head_masks[h], ck["at"][:, sls[hp]], 0.0),
                                 jnp.where(head_masks[h], ck["rt"][:, sls[hp]], 0.0)], axis=0)
            gb.append(_dot_nt(x, ck["bt"][:, sls[hp]]))
            gk.append(_dot_nt(x, ck["kt"][:, sls[hp]]))
        aab = [jnp.where(strict, m[:C], 0.0) for m in gb]
        aak = [jnp.where(strict, m[:C], 0.0) for m in gk]
        arb = [jnp.where(incl, m[C:], 0.0) for m in gb]
        ark = [jnp.where(incl, m[C:], 0.0) for m in gk]
        tinv = [eye + jnp.where(level_masks[0], n, 0.0) for n in aab]
        for lmask in level_masks[1:]:
            tmp = [_dot(t, jnp.where(lmask, n, 0.0)) for t, n in zip(tinv, aab)]
            tinv = [t + _dot(x, t) for t, x in zip(tinv, tmp)]
        av = [_dot(aak[i], ck["vv"][:, sls[hp]]) for i, (ck, hp, h) in enumerate(units)]
        zz = [jnp.concatenate([ck["at"][:, sls[hp]], jnp.where(m0, av[2 * j], av[2 * j + 1])], axis=1)
              for j, (ck, hp) in enumerate(pairs)]
        tz = [_dot(tinv[i], zz[i // 2]) for i in range(len(units))]
        ahat = [jnp.where(m0, tz[2 * j][:, :LANES], tz[2 * j + 1][:, :LANES]) for j in range(len(pairs))]
        wmat = [jnp.where(m0, tz[2 * j][:, LANES:], tz[2 * j + 1][:, LANES:]) for j in range(len(pairs))]
        aw = [jnp.concatenate([ahat[j], wmat[j]], axis=1) for j in range(len(pairs))]
        ry = [_dot(arb[i], aw[i // 2]) for i in range(len(units))]
        akv = [_dot(ark[i], ck["vv"][:, sls[hp]]) for i, (ck, hp, h) in enumerate(units)]
        rhat_e, yi, mp, np_ = [], [], [], []
        for j, (ck, hp) in enumerate(pairs):
            sl = sls[hp]
            em_p, g_p = ck["em"][:, sl], ck["g"][:, sl]
            bg = ck["bt"][:, sl] * g_p
            kg = ck["kt"][:, sl] * g_p
            rhat_e.append((ck["rt"][:, sl] + jnp.where(m0, ry[2 * j][:, :LANES], ry[2 * j + 1][:, :LANES])) * em_p)
            yi.append(jnp.where(m0, ry[2 * j][:, LANES:] + akv[2 * j], ry[2 * j + 1][:, LANES:] + akv[2 * j + 1]))
            mp.append(jnp.where(pair_diag, _dot_tn(ahat[j] * em_p, bg), 0.0))
            np_.append(jnp.where(pair_diag, _dot_tn(jnp.concatenate([wmat[j], ck["vv"][:, sl]], axis=0),
                                                    jnp.concatenate([bg, kg], axis=0)), 0.0))
        for n in range(NSEQ):
            for hp in range(N_PAIRS):
                s_bd = st_s[n * N_PAIRS + hp]
                for ci, ck in enumerate(cks):
                    if ck["seq"] != n:
                        continue
                    j = ci * N_PAIRS + hp
                    y_s[ck["rows"], sls[hp]] = _dot_nt(rhat_e[j], s_bd) + yi[j]
                    s_bd = s_bd * ck["pc"][:, sls[hp]] + _dot(s_bd, mp[j]) + np_[j]
                st_s[n * N_PAIRS + hp] = s_bd
        return carry

    lax.fori_loop(0, NCH // CPI, chunk_group, 0)

    for sl in sls:
        y = y_s[:, sl]
        mean = _head_sum(y, ones2) * (1.0 / HEAD)
        d = y - mean
        var = _head_sum(d * d, ones2) * (1.0 / HEAD)
        out = d * lax.rsqrt(var + GN_EPS) * gng_ref[:, sl] + gnb_ref[:, sl] + bo_s[:, sl]
        for n in range(NSEQ):
            y_ref[n, :, sl] = out[n * TB:(n + 1) * TB]

    @pl.when(tb == pl.num_programs(1) - 1)
    def _():
        for n in range(NSEQ):
            for hp in range(N_PAIRS):
                for h in range(2):
                    sout_ref[n, 2 * hp + h] = lax.dot_general(
                        st_s[n * N_PAIRS + hp, h * HEAD:(h + 1) * HEAD, :], place[h], (((1,), (1,)), ((), ())),
                        precision=HIGHEST, preferred_element_type=F32)


def _wkv_call(p, shift0, s0, rw, *, C, TB, CPI, NSEQ=1):
    B, T, _ = p.shape
    H = 2 * N_PAIRS
    assert NSEQ == 1 or (T == TB and TB // C == CPI)
    return pl.pallas_call(
        functools.partial(_wkv_kernel, C, TB // C, CPI, NSEQ),
        grid=(B // NSEQ, T // TB),
        in_specs=[
            pl.BlockSpec((NSEQ, TB, A_SHIFT_W), lambda b, t: (b, t, 0)),
            pl.BlockSpec((NSEQ, 8, A_SHIFT_W), lambda b, t: (b, jnp.maximum(t * (TB // 8) - 1, 0), 0)),
            pl.BlockSpec((NSEQ, 1, A_SHIFT_W), lambda b, t: (b, 0, 0)),
            pl.BlockSpec((NSEQ, H, HEAD, HEAD), lambda b, t: (b, 0, 0, 0)),
        ] + [_full_spec(x.shape) for x in rw],
        out_specs=[
            pl.BlockSpec((NSEQ, TB, A_WIDTH), lambda b, t: (b, t, 0)),
            pl.BlockSpec((NSEQ, H, HEAD, HEAD), lambda b, t: (b, 0, 0, 0)),
        ],
        out_shape=[jax.ShapeDtypeStruct((B, T, A_WIDTH), F32),
                   jax.ShapeDtypeStruct((B, H, HEAD, HEAD), F32)],
        scratch_shapes=[pltpu.VMEM((NSEQ * TB, A_WIDTH), F32) for _ in range(8)]
        + [pltpu.VMEM((NSEQ * N_PAIRS, LANES, LANES), F32)],
        compiler_params=_params(("parallel", "arbitrary")),
        name="wkv7_chunked",
    )(p, p, shift0, s0, *rw)


def _even_out_kernel(ya_ref, pb_ref, pbprev_ref, u0_ref, x_ref, wout_ref, cw_ref, g_ref, b_ref,
                     o_ref, ulast_ref):
    tb = pl.program_id(1)
    TB = ya_ref.shape[1]
    pb = pb_ref[0].astype(F32)
    za, bb, cc, hh, zb = (pb[:, i * 512:(i + 1) * 512] for i in range(5))
    u = cc * hh
    n_prev = pbprev_ref.shape[1]
    tail = pbprev_ref[0].astype(F32)[n_prev - 2:n_prev]
    tail_u = tail[:, 1024:1536] * tail[:, 1536:2048]
    up2 = jnp.where(tb == 0, u0_ref[0, 6:7], tail_u[0:1])
    up1 = jnp.where(tb == 0, u0_ref[0, 7:8], tail_u[1:2])
    row = lax.broadcasted_iota(jnp.int32, (TB, 1), 0)
    u1 = jnp.where(row == 0, up1, pltpu.roll(u, 1, axis=0))
    u2 = jnp.where(row == 0, up2, jnp.where(row == 1, up1, pltpu.roll(u, 2, axis=0)))
    cw = cw_ref[...]
    yb = bb * (cw[0:1] * u2 + cw[1:2] * u1 + cw[2:3] * u)
    mix_a = ya_ref[0] * _silu(za)
    mix_b = yb * _silu(zb)
    y = _dot(mix_a, wout_ref[0:A_WIDTH, :]) + _dot(mix_b, wout_ref[A_WIDTH:, :])
    o_ref[0] = _layer_norm(DEEPNORM_ALPHA * x_ref[0] + y, g_ref[...], b_ref[...])
    ulast_ref[0] = u[TB - 8:TB]


def _even_out_call(ya, pb, u0, x, wout, cw, g, b, TB):
    B, T, _ = x.shape
    n_prev = min(16, T)
    return pl.pallas_call(
        _even_out_kernel,
        grid=(B, T // TB),
        in_specs=[
            pl.BlockSpec((1, TB, A_WIDTH), lambda b_, t: (b_, t, 0)),
            pl.BlockSpec((1, TB, EVEN_REST_W), lambda b_, t: (b_, t, 0)),
            pl.BlockSpec((1, n_prev, EVEN_REST_W), lambda b_, t: (b_, jnp.maximum(t * (TB // n_prev) - 1, 0), 0)),
            pl.BlockSpec((1, 8, B_WIDTH), lambda b_, t: (b_, 0, 0)),
            pl.BlockSpec((1, TB, D_MODEL), lambda b_, t: (b_, t, 0)),
            _full_spec(wout.shape), _full_spec(cw.shape), _full_spec(g.shape), _full_spec(b.shape),
        ],
        out_specs=[pl.BlockSpec((1, TB, D_MODEL), lambda b_, t: (b_, t, 0)),
                   pl.BlockSpec((1, 8, B_WIDTH), lambda b_, t: (b_, 0, 0))],
        out_shape=[jax.ShapeDtypeStruct((B, T, D_MODEL), F32), jax.ShapeDtypeStruct((B, 8, B_WIDTH), F32)],
        compiler_params=_params(("parallel", "arbitrary")),
        name="even_out",
    )(ya, pb, pb, u0, x, wout, cw, g, b)


def _attn_prompt_kernel(*refs):
    q_refs, k_refs, v_refs = refs[0:3], refs[3:6], refs[6:9]
    o_ref, og_s, lse_s = refs[9], refs[10], refs[11]
    BLK = ATTN_BLOCK
    lane = lax.broadcasted_iota(jnp.int32, (1, LANES), 1)
    m0 = lane < HEAD
    qi = lax.broadcasted_iota(jnp.int32, (2 * BLK, 2 * BLK), 0) % BLK
    kj = lax.broadcasted_iota(jnp.int32, (2 * BLK, 2 * BLK), 1)
    band = (kj >= qi) & (kj <= qi + BLK)
    qi1 = lax.broadcasted_iota(jnp.int32, (2 * BLK, BLK), 0) % BLK
    kj1 = lax.broadcasted_iota(jnp.int32, (2 * BLK, BLK), 1)
    causal = kj1 <= qi1
    for g, (window, d) in enumerate(C_GROUPS):
        assert window // d == BLK
        nb = q_refs[g].shape[1] // (d * BLK)
        for rho in range(d):
            k_prev = v_prev = None
            for i in range(nb):
                start = rho + d * BLK * i
                rows = pl.ds(start, BLK, stride=d) if d > 1 else pl.ds(start, BLK)
                q = q_refs[g][0, rows, :] * (HEAD ** -0.5)
                k_cur = k_refs[g][0, rows, :].astype(BF16)
                v_cur = v_refs[g][0, rows, :].astype(BF16)
                if i == 0:
                    keys, vals, mask = k_cur, v_cur, causal
                else:
                    keys = jnp.concatenate([k_prev, k_cur], axis=0)
                    vals = jnp.concatenate([v_prev, v_cur], axis=0)
                    mask = band
                q2 = jnp.concatenate([jnp.where(m0, q, 0.0), jnp.where(m0, 0.0, q)], axis=0)
                s = jnp.where(mask, _dot_nt(q2, keys), NEG_BIG)
                mx = jnp.max(s, axis=-1, keepdims=True)
                e = jnp.exp(s - mx)
                den = jnp.sum(e, axis=-1, keepdims=True)
                o2 = _dot(e, vals) * (1.0 / den)
                lse2 = mx + jnp.log(den)
                og_s[g, rows, :] = jnp.where(m0, o2[:BLK], o2[BLK:])
                lse_s[g, rows, :] = jnp.where(m0, lse2[:BLK], lse2[BLK:])
                k_prev, v_prev = k_cur, v_cur
    l0, l1, l2 = lse_s[0], lse_s[1], lse_s[2]
    mx = jnp.maximum(jnp.maximum(l0, l1), l2)
    w0, w1, w2 = jnp.exp(l0 - mx), jnp.exp(l1 - mx), jnp.exp(l2 - mx)
    o_ref[0] = (w0 * og_s[0] + w1 * og_s[1] + w2 * og_s[2]) * (1.0 / (w0 + w1 + w2))


def _attn_prompt_call(p):
    B, T, _ = p.shape
    n_g = len(C_GROUPS)

    def col_spec(part, g):
        base = (part * n_g + g) * C_WIDTH // LANES
        return pl.BlockSpec((1, T, LANES), lambda b, hp: (b, 0, base + hp))

    specs = [col_spec(part, g) for part in range(3) for g in range(n_g)]
    return pl.pallas_call(
        _attn_prompt_kernel,
        grid=(B, N_PAIRS),
        in_specs=specs,
        out_specs=pl.BlockSpec((1, T, LANES), lambda b, hp: (b, 0, hp)),
        out_shape=jax.ShapeDtypeStruct((B, T, C_WIDTH), F32),
        scratch_shapes=[pltpu.VMEM((n_g, T, LANES), F32), pltpu.VMEM((n_g, T, LANES), F32)],
        compiler_params=_params(("parallel", "parallel")),
        name="attn_prompt",
    )(*([p] * 9))


def _attn_sample_kernel(n_alias, ps_ref, *refs):
    c_refs = refs[0:3]
    o_ref = refs[3 + n_alias]
    kv_refs = refs[4 + n_alias:7 + n_alias]
    T = ps_ref.shape[1]
    p = ps_ref[0]
    lane = lax.broadcasted_iota(jnp.int32, (1, LANES), 1)
    m0 = lane < HEAD
    tq = lax.broadcasted_iota(jnp.int32, (2 * T, 1), 0) % T
    for hp in range(N_PAIRS):
        m_run = jnp.full((2 * T, 1), NEG_BIG, F32)
        l_run = jnp.zeros((2 * T, 1), F32)
        acc = jnp.zeros((2 * T, LANES), F32)
        ch = slice(hp * LANES, (hp + 1) * LANES)
        for g, (window, d) in enumerate(C_GROUPS):
            wb = c_refs[g].shape[4]
            assert wb == window and window % d == 0
            col = g * C_WIDTH + hp * LANES
            q = p[:, col:col + LANES] * (HEAD ** -0.5)
            k_new = p[:, C_QKV_W // 3 + col:C_QKV_W // 3 + col + LANES]
            v_new = p[:, 2 * C_QKV_W // 3 + col:2 * C_QKV_W // 3 + col + LANES]
            q2 = jnp.concatenate([jnp.where(m0, q, 0.0), jnp.where(m0, 0.0, q)], axis=0)
            pos = lax.broadcasted_iota(jnp.int32, (2 * T, wb), 1)
            ok_old = (pos >= tq + (wb - window)) & (((pos - tq) & (d - 1)) == 0)
            tn = lax.broadcasted_iota(jnp.int32, (2 * T, T), 1)
            ok_new = (tn <= tq) & (((tq - tn) & (d - 1)) == 0)
            s_old = jnp.where(ok_old, _dot(q2, c_refs[g][0, 0, 0, ch, :]), NEG_BIG)
            s_new = jnp.where(ok_new, _dot_nt(q2, k_new), NEG_BIG)
            m_new = jnp.maximum(m_run, jnp.maximum(jnp.max(s_old, axis=-1, keepdims=True),
                                                   jnp.max(s_new, axis=-1, keepdims=True)))
            e_old = jnp.exp(s_old - m_new)
            e_new = jnp.exp(s_new - m_new)
            scale = jnp.exp(m_run - m_new)
            l_run = l_run * scale + jnp.sum(e_old, axis=-1, keepdims=True) + jnp.sum(e_new, axis=-1, keepdims=True)
            acc = acc * scale + _dot_nt(e_old, c_refs[g][0, 0, 1, ch, :]) + _dot(e_new, v_new)
            m_run = m_new
        o2 = acc * (1.0 / l_run)
        o_ref[0, :, ch] = jnp.where(m0, o2[:T], o2[T:])
    tail = lane >= LANES - T
    for g in range(len(C_GROUPS)):
        wb = c_refs[g].shape[4]
        for part in range(2):
            c0 = (part + 1) * C_QKV_W // 3 + g * C_WIDTH
            new_rows = jnp.concatenate([jnp.zeros((LANES - T, C_WIDTH), F32), p[:, c0:c0 + C_WIDTH]], axis=0)
            new_t = new_rows.T
            for rb in range(C_WIDTH // LANES):
                rs = slice(rb * LANES, (rb + 1) * LANES)
                rolled = pltpu.roll(c_refs[g][0, 0, part, rs, :], wb - T, axis=1)
                if wb > LANES:
                    kv_refs[g][0, 0, part, rs, 0:wb - LANES] = rolled[:, 0:wb - LANES]
                kv_refs[g][0, 0, part, rs, wb - LANES:wb] = jnp.where(tail, new_t[rs], rolled[:, wb - LANES:wb])


def _attn_sample_call(p, caches, layer_slot, kv_prev):
    B, T, _ = p.shape
    cache_spec = lambda c: pl.BlockSpec((1, 1) + c.shape[2:], lambda b: (layer_slot, b, 0, 0, 0))
    in_specs = [pl.BlockSpec((1, T, ODD_IN_W), lambda b: (b, 0, 0))] + [cache_spec(c) for c in caches]
    args = [p] + list(caches)
    aliases = {}
    n_alias = 0
    if kv_prev is not None:
        n_alias = 3
        in_specs += [pl.BlockSpec(memory_space=pl.ANY)] * 3
        args += list(kv_prev)
        aliases = {4: 1, 5: 2, 6: 3}
    outs = pl.pallas_call(
        functools.partial(_attn_sample_kernel, n_alias),
        grid=(B,),
        in_specs=in_specs,
        out_specs=[pl.BlockSpec((1, T, C_WIDTH), lambda b: (b, 0, 0))] + [cache_spec(c) for c in caches],
        out_shape=[jax.ShapeDtypeStruct((B, T, C_WIDTH), F32)]
        + [jax.ShapeDtypeStruct(c.shape, F32) for c in caches],
        input_output_aliases=aliases,
        compiler_params=_params(("parallel",)),
        name="attn_sample",
    )(*args)
    return outs[0], outs[1:]


def _odd_out_kernel(o_ref, z_ref, x_ref, wout_ref, g_ref, b_ref, out_ref):
    y = _dot(o_ref[...] * _silu(z_ref[...]), wout_ref[...])
    out_ref[...] = _layer_norm(DEEPNORM_ALPHA * x_ref[...] + y, g_ref[...], b_ref[...])


def _odd_out_call(o, p, x, wout, g, b, tm):
    B, T, D = x.shape
    M = B * T
    out = pl.pallas_call(
        _odd_out_kernel,
        grid=(M // tm,),
        in_specs=[
            pl.BlockSpec((tm, C_WIDTH), lambda i: (i, 0)),
            pl.BlockSpec((tm, C_WIDTH), lambda i: (i, C_QKV_W // C_WIDTH)),
            pl.BlockSpec((tm, D), lambda i: (i, 0)),
            _full_spec(wout.shape), _full_spec(g.shape), _full_spec(b.shape),
        ],
        out_specs=pl.BlockSpec((tm, D), lambda i: (i, 0)),
        out_shape=jax.ShapeDtypeStruct((M, D), F32),
        compiler_params=_params(("parallel",)),
        name="odd_out",
    )(o.reshape(M, C_WIDTH), p.reshape(M, ODD_IN_W), x.reshape(M, D), wout, g, b)
    return out.reshape(B, T, D)


def kernel(x_prompt, x_sample, state_rwkv, state_shift, state_conv, cache_kv_w128, cache_kv_w512, cache_kv_w2048, even_w_in, even_w_out, rwkv_mu, rwkv_w0, rwkv_w2, rwkv_a0, rwkv_a2, rwkv_k_k, rwkv_k_a, rwkv_r_k, rwkv_gn_g, rwkv_gn_b, conv_w, odd_w_in, odd_w_out, ln_g, ln_b):
    xp, xs = x_prompt, x_sample
    BP, TP, _ = xp.shape
    BS, TS, _ = xs.shape
    row = lambda t: t.reshape(1, -1)
    to_channel_major = lambda c: jnp.transpose(c, (0, 1, 3, 4, 5, 2)).reshape(
        c.shape[0], c.shape[1], 2, C_WIDTH, c.shape[2])
    caches = [to_channel_major(c) for c in (cache_kv_w128, cache_kv_w512, cache_kv_w2048)]
    rwkv_p, rwkv_s, shift_p, shift_s, conv_p, conv_s = [], [], [], [], [], []
    kv_p = kv_s = None
    for li in range(DEPTH):
        j = li // 2
        g, b = row(ln_g[li]), row(ln_b[li])
        if li % 2 == 0:
            w_in = even_w_in[j].astype(BF16)
            w_a, w_rest = w_in[:, :A_SHIFT_W], w_in[:, A_SHIFT_W:]
            w_out = even_w_out[j].astype(BF16)
            zeros_lora = jnp.zeros((HEAD, A_WIDTH), F32)
            rw = (row(rwkv_mu[j]), row(rwkv_w0[j]), _stack_split(jnp.concatenate([rwkv_w2[j], zeros_lora], 0)),
                  row(rwkv_a0[j]), _stack_split(jnp.concatenate([zeros_lora, rwkv_a2[j]], 0)), row(rwkv_k_k[j]),
                  row(rwkv_k_a[j]), row(rwkv_r_k[j]), row(rwkv_gn_g[j]), row(rwkv_gn_b[j]))
            pa, pb = _proj_call(xp, [w_a, w_rest], tm=512, out_dtypes=[F32, BF16])
            ya, s_new = _wkv_call(pa, jnp.zeros((BP, 1, A_SHIFT_W), F32),
                                  jnp.zeros((BP, 2 * N_PAIRS, HEAD, HEAD), F32), rw,
                                  C=WKV_CHUNK, TB=8 * WKV_CHUNK, CPI=4)
            xp, ulast = _even_out_call(ya, pb, jnp.zeros((BP, 8, B_WIDTH), F32), xp, w_out, conv_w[j], g, b,
                                       TB=4 * WKV_CHUNK)
            rwkv_p.append(s_new)
            shift_p.append(pa[:, -1, :])
            conv_p.append(ulast[:, 6:8, :])
            pa, pb = _proj_call(xs, [w_a, w_rest], tm=BS * TS, out_dtypes=[F32, BF16])
            ya, s_new = _wkv_call(pa, state_shift[j].reshape(BS, 1, A_SHIFT_W), state_rwkv[j], rw,
                                  C=TS, TB=TS, CPI=1, NSEQ=4)
            u0 = jnp.concatenate([jnp.zeros((BS, 6, B_WIDTH), F32), state_conv[j]], axis=1)
            xs, ulast = _even_out_call(ya, pb, u0, xs, w_out, conv_w[j], g, b, TB=TS)
            rwkv_s.append(s_new)
            shift_s.append(pa[:, -1, :])
            conv_s.append(ulast[:, 6:8, :])
        else:
            w_in = odd_w_in[j].astype(BF16)
            w_out = odd_w_out[j].astype(BF16)
            pp, kv_p = _proj_odd_prompt_call(xp, w_in, j, kv_p)
            op = _attn_prompt_call(pp)
            xp = _odd_out_call(op, pp, xp, w_out, g, b, tm=512)
            (ps,) = _proj_call(xs, [w_in], tm=BS * TS)
            os_, kv_s = _attn_sample_call(ps, caches, j, kv_s)
            xs = _odd_out_call(os_, ps, xs, w_out, g, b, tm=BS * TS)
    kv_shape = lambda t: jnp.transpose(
        t.reshape(t.shape[0], t.shape[1], 2, C_WIDTH // HEAD, HEAD, t.shape[4]), (0, 1, 5, 2, 3, 4))
    return (xp, xs,
            jnp.stack(rwkv_p), jnp.stack(rwkv_s),
            jnp.stack(shift_p), jnp.stack(shift_s),
            jnp.stack(conv_p), jnp.stack(conv_s),
            kv_shape(kv_p[0]), kv_shape(kv_s[0]),
            kv_shape(kv_p[1]), kv_shape(kv_s[1]),
            kv_shape(kv_p[2]), kv_shape(kv_s[2]))
```

```python
import functools

import jax
import jax.numpy as jnp
from jax import lax
from jax.experimental import pallas as pl
from jax.experimental.pallas import tpu as pltpu

F32 = jnp.float32
BF16 = jnp.bfloat16
HIGHEST = lax.Precision.HIGHEST

D_MODEL = 1024
DEPTH = 4
HEAD = 64
LANES = 128
A_WIDTH = 512
N_PAIRS = A_WIDTH // LANES
A_SHIFT_W = 3 * A_WIDTH + 2 * HEAD
B_WIDTH = 512
EVEN_REST_W = A_WIDTH + 4 * B_WIDTH
C_GROUPS = ((128, 1), (512, 4), (2048, 16))
C_WIDTH = 512
C_QKV_W = 3 * len(C_GROUPS) * C_WIDTH
ODD_IN_W = C_QKV_W + C_WIDTH
ATTN_BLOCK = 128
GN_EPS = 64e-5
LN_EPS = 1e-5
DEEPNORM_ALPHA = (2 * DEPTH) ** 0.25
NEG_BIG = -1e30
WKV_CHUNK = 64
VMEM_LIMIT = 56 * 1024 * 1024


def _dot(a, b):
    return jnp.dot(a.astype(BF16), b.astype(BF16), preferred_element_type=F32)


def _dot_nt(a, b):
    return lax.dot_general(a.astype(BF16), b.astype(BF16), (((1,), (1,)), ((), ())),
                           preferred_element_type=F32)


def _dot_tn(a, b):
    return lax.dot_general(a.astype(BF16), b.astype(BF16), (((0,), (0,)), ((), ())),
                           preferred_element_type=F32)


def _dot_f32(a, b):
    return jnp.dot(a, b, precision=HIGHEST, preferred_element_type=F32)


def _split_bf16(x):
    hi = x.astype(BF16)
    return hi, (x - hi.astype(F32)).astype(BF16)


def _dot_split(a, b3):
    hi, lo = _split_bf16(a)
    return jnp.dot(jnp.concatenate([hi, hi, lo], axis=1), b3, preferred_element_type=F32)


def _stack_split(b):
    hi, lo = _split_bf16(b)
    return jnp.concatenate([hi, lo, hi], axis=0)


def _silu(z):
    return z * (1.0 / (1.0 + jnp.exp(-z)))


def _layer_norm(xf, g, b):
    mu = jnp.mean(xf, axis=-1, keepdims=True)
    d = xf - mu
    var = jnp.mean(d * d, axis=-1, keepdims=True)
    return d * lax.rsqrt(var + LN_EPS) * g + b


def _full_spec(shape):
    return pl.BlockSpec(shape, lambda *_: (0,) * len(shape))


def _params(semantics):
    return pltpu.CompilerParams(dimension_semantics=semantics, vmem_limit_bytes=VMEM_LIMIT)


def _proj_kernel(n_out, x_ref, *refs):
    w_refs, o_refs = refs[:n_out], refs[n_out:]
    xb = x_ref[...].astype(BF16)
    for w_ref, o_ref in zip(w_refs, o_refs):
        o_ref[...] = jnp.dot(xb, w_ref[...], preferred_element_type=F32).astype(o_ref.dtype)


def _proj_call(x, weights, tm, out_dtypes=None):
    B, T, D = x.shape
    M = B * T
    out_dtypes = out_dtypes or [F32] * len(weights)
    outs = pl.pallas_call(
        functools.partial(_proj_kernel, len(weights)),
        grid=(M // tm,),
        in_specs=[pl.BlockSpec((tm, D), lambda i: (i, 0))] + [_full_spec(w.shape) for w in weights],
        out_specs=[pl.BlockSpec((tm, w.shape[1]), lambda i: (i, 0)) for w in weights],
        out_shape=[jax.ShapeDtypeStruct((M, w.shape[1]), dt) for w, dt in zip(weights, out_dtypes)],
        compiler_params=_params(("parallel",)),
        name="in_proj",
    )(x.reshape(M, D), *weights)
    return [o.reshape(B, T, -1) for o in outs]


def _kept_rows(window, T, tm):
    rows = min(window, tm)
    return rows, min(window, T) // rows


def _proj_odd_prompt_kernel(tm, T, x_ref, w_ref, *refs):
    p_ref, kv_refs = refs[-4], refs[-3:]
    i = pl.program_id(1)
    nt = T // tm
    p_ref[0] = jnp.dot(x_ref[0].astype(BF16), w_ref[...], preferred_element_type=F32)
    for g, ((window, _), kv_ref) in enumerate(zip(C_GROUPS, kv_refs)):
        rows, kt = _kept_rows(window, T, tm)

        @pl.when(i >= nt - kt)
        def _():
            for part in range(2):
                c0 = (part + 1) * C_QKV_W // 3 + g * C_WIDTH
                kv_ref[0, 0, part] = p_ref[0, tm - rows:tm, c0:c0 + C_WIDTH].T


def _proj_odd_prompt_call(x, w, layer_slot, kv_prev, tm=512):
    B, T, D = x.shape
    nt = T // tm
    n_odd = DEPTH // 2
    kv_specs, kv_shapes = [], []
    for window, _ in C_GROUPS:
        rows, kt = _kept_rows(window, T, tm)

        def index_map(b, i, kt=kt):
            return (layer_slot, b, 0, 0, jnp.maximum(i - (nt - kt), 0))

        kv_specs.append(pl.BlockSpec((1, 1, 2, C_WIDTH, rows), index_map))
        kv_shapes.append(jax.ShapeDtypeStruct((n_odd, B, 2, C_WIDTH, rows * kt), F32))
    in_specs = [pl.BlockSpec((1, tm, D), lambda b, i: (b, i, 0)), _full_spec(w.shape)]
    args = [x, w]
    aliases = {}
    if kv_prev is not None:
        in_specs += [pl.BlockSpec(memory_space=pl.ANY)] * 3
        args += list(kv_prev)
        aliases = {2: 1, 3: 2, 4: 3}
    outs = pl.pallas_call(
        functools.partial(_proj_odd_prompt_kernel, tm, T),
        grid=(B, nt),
        in_specs=in_specs,
        out_specs=[pl.BlockSpec((1, tm, ODD_IN_W), lambda b, i: (b, i, 0))] + kv_specs,
        out_shape=[jax.ShapeDtypeStruct((B, T, ODD_IN_W), F32)] + kv_shapes,
        input_output_aliases=aliases,
        compiler_params=_params(("parallel", "arbitrary")),
        name="in_proj_attn_prompt",
    )(*args)
    return outs[0], outs[1:]


def _head_sum(x, ones2):
    xh = x.astype(BF16)
    xl = (x - xh.astype(F32)).astype(BF16)
    return jnp.dot(jnp.concatenate([xh, xl], axis=1), ones2, preferred_element_type=F32)


def _prefix_sum(x, n):
    row = lax.broadcasted_iota(jnp.int32, (n, 1), 0)
    s = 1
    while s < n:
        x = x + jnp.where(row >= s, pltpu.roll(x, s, axis=0), 0.0)
        s *= 2
    return x


def _wkv_kernel(C, NCH, CPI, NSEQ, p_ref, pprev_ref, shift0_ref, s0_ref, mu_ref, w0_ref, w2p_ref, a0_ref, a2p_ref,
                kk_ref, ka_ref, rk_ref, gng_ref, gnb_ref,
                y_ref, sout_ref,
                r_s, k_s, v_s, al_s, be_s, lw_s, bo_s, y_s, st_s):
    tb = pl.program_id(1)
    TB = C * NCH
    ROWS = NSEQ * TB

    ei = lax.broadcasted_iota(jnp.int32, (HEAD, LANES), 0)
    ej = lax.broadcasted_iota(jnp.int32, (HEAD, LANES), 1)
    place = [jnp.where(ej == ei + h * HEAD, 1.0, 0.0).astype(F32) for h in range(2)]

    @pl.when(tb == 0)
    def _():
        for n in range(NSEQ):
            for hp in range(N_PAIRS):
                for h in range(2):
                    st_s[n * N_PAIRS + hp, h * HEAD:(h + 1) * HEAD, :] = _dot_f32(s0_ref[n, 2 * hp + h], place[h])

    p = jnp.concatenate([p_ref[n] for n in range(NSEQ)], axis=0)
    first_rows = [jnp.where(tb == 0, shift0_ref[n], pprev_ref[n, 7:8, :]) for n in range(NSEQ)]
    first_rows = first_rows[0] if NSEQ == 1 else jnp.concatenate(
        [jnp.broadcast_to(f, (TB, A_SHIFT_W)) for f in first_rows], axis=0)
    row = lax.broadcasted_iota(jnp.int32, (ROWS, 1), 0)
    prev = jnp.where(row % TB == 0, first_rows, pltpu.roll(p, 1, axis=0))
    ps = p + (prev - p) * mu_ref[...]
    r = ps[:, 0:A_WIDTH]
    k = ps[:, A_WIDTH:2 * A_WIDTH]
    v = ps[:, 2 * A_WIDTH:3 * A_WIDTH]
    lora_in = ps[:, 3 * A_WIDTH:A_SHIFT_W]
    nw = -(w0_ref[...] + _dot_split(jnp.tanh(lora_in), w2p_ref[...]))
    softplus = jnp.maximum(nw, 0.0) + jnp.log(1.0 + jnp.exp(-jnp.abs(nw)))
    lw = -jnp.exp(-softplus - 0.5)
    a = 1.0 / (1.0 + jnp.exp(-(a0_ref[...] + _dot_split(lora_in, a2p_ref[...]))))

    li = lax.broadcasted_iota(jnp.int32, (2 * LANES, LANES), 0)
    lj = lax.broadcasted_iota(jnp.int32, (2 * LANES, LANES), 1)
    ones2 = jnp.where(((li % LANES) // HEAD) == (lj // HEAD), 1.0, 0.0).astype(BF16)

    kk = k * kk_ref[...]
    kmod = k * (1.0 + (a - 1.0) * ka_ref[...])
    rk = r * kmod * rk_ref[...]
    sls = [slice(hp * LANES, (hp + 1) * LANES) for hp in range(N_PAIRS)]
    for sl in sls:
        kkp = kk[:, sl]
        kkn = kkp * lax.rsqrt(jnp.maximum(_head_sum(kkp * kkp, ones2), 1e-24))
        al_s[:, sl] = -kkn
        be_s[:, sl] = kkn * a[:, sl]
        bo_s[:, sl] = _head_sum(rk[:, sl], ones2) * v[:, sl]
    r_s[...] = r
    k_s[...] = kmod
    v_s[...] = v
    lw_s[...] = lw

    ti = lax.broadcasted_iota(jnp.int32, (C, C), 0)
    tj = lax.broadcasted_iota(jnp.int32, (C, C), 1)
    strict = ti > tj
    incl = ti >= tj
    eye = jnp.where(ti == tj, 1.0, 0.0).astype(F32)
    level_masks = []
    b = 1
    while b < C:
        level_masks.append(((ti // (2 * b)) == (tj // (2 * b))) & (((ti // b) % 2) == 1) & (((tj // b) % 2) == 0))
        b *= 2
    lane = lax.broadcasted_iota(jnp.int32, (1, LANES), 1)
    head_masks = (lane < HEAD, lane >= HEAD)
    m0 = head_masks[0]
    bi = lax.broadcasted_iota(jnp.int32, (LANES, LANES), 0)
    bj = lax.broadcasted_iota(jnp.int32, (LANES, LANES), 1)
    pair_diag = (bi // HEAD) == (bj // HEAD)

    def chunk_group(it, carry):
        cks = []
        for n, ci in [(n, ci) for n in range(NSEQ) for ci in range(CPI)]:
            rows = pl.ds(pl.multiple_of(n * TB + (it * CPI + ci) * C, C), C)
            lw_c = lw_s[rows, :]
            linc = _prefix_sum(lw_c, C)
            lm = linc[C // 2 - 1:C // 2, :]
            lend = linc[C - 1:C, :]
            e_fwd = jnp.exp(linc - lm)
            e_bwd = jnp.exp(lm - linc)
            cks.append(dict(
                rows=rows, seq=n,
                at=al_s[rows, :] * jnp.exp(linc - lw_c - lm),
                rt=r_s[rows, :] * e_fwd,
                bt=be_s[rows, :] * e_bwd,
                kt=k_s[rows, :] * e_bwd,
                vv=v_s[rows, :],
                em=jnp.exp(lm), g=jnp.exp(lend - lm), pc=jnp.exp(lend)))
        units = [(ck, hp, h) for ck in cks for hp in range(N_PAIRS) for h in range(2)]
        pairs = [(ck, hp) for ck in cks for hp in range(N_PAIRS)]
        gb, gk = [], []
        for ck, hp, h in units:
            x = jnp.concatenate([jnp.where(head_masks[h], ck["at"][:, sls[hp]], 0.0),
                                 jnp.where(head_masks[h], ck["rt"][:, sls[hp]], 0.0)], axis=0)
            gb.append(_dot_nt(x, ck["bt"][:, sls[hp]]))
            gk.append(_dot_nt(x, ck["kt"][:, sls[hp]]))
        aab = [jnp.where(strict, m[:C], 0.0) for m in gb]
        aak = [jnp.where(strict, m[:C], 0.0) for m in gk]
        arb = [jnp.where(incl, m[C:], 0.0) for m in gb]
        ark = [jnp.where(incl, m[C:], 0.0) for m in gk]
        tinv = [eye + jnp.where(level_masks[0], n, 0.0) for n in aab]
        for lmask in level_masks[1:]:
            tmp = [_dot(t, jnp.where(lmask, n, 0.0)) for t, n in zip(tinv, aab)]
            tinv = [t + _dot(x, t) for t, x in zip(tinv, tmp)]
        av = [_dot(aak[i], ck["vv"][:, sls[hp]]) for i, (ck, hp, h) in enumerate(units)]
        zz = [jnp.concatenate([ck["at"][:, sls[hp]], jnp.where(m0, av[2 * j], av[2 * j + 1])], axis=1)
              for j, (ck, hp) in enumerate(pairs)]
        tz = [_dot(tinv[i], zz[i // 2]) for i in range(len(units))]
        ahat = [jnp.where(m0, tz[2 * j][:, :LANES], tz[2 * j + 1][:, :LANES]) for j in range(len(pairs))]
        wmat = [jnp.where(m0, tz[2 * j][:, LANES:], tz[2 * j + 1][:, LANES:]) for j in range(len(pairs))]
        aw = [jnp.concatenate([ahat[j], wmat[j]], axis=1) for j in range(len(pairs))]
        ry = [_dot(arb[i], aw[i // 2]) for i in range(len(units))]
        akv = [_dot(ark[i], ck["vv"][:, sls[hp]]) for i, (ck, hp, h) in enumerate(units)]
        rhat_e, yi, mp, np_ = [], [], [], []
        for j, (ck, hp) in enumerate(pairs):
            sl = sls[hp]
            em_p, g_p = ck["em"][:, sl], ck["g"][:, sl]
            bg = ck["bt"][:, sl] * g_p
            kg = ck["kt"][:, sl] * g_p
            rhat_e.append((ck["rt"][:, sl] + jnp.where(m0, ry[2 * j][:, :LANES], ry[2 * j + 1][:, :LANES])) * em_p)
            yi.append(jnp.where(m0, ry[2 * j][:, LANES:] + akv[2 * j], ry[2 * j + 1][:, LANES:] + akv[2 * j + 1]))
            mp.append(jnp.where(pair_diag, _dot_tn(ahat[j] * em_p, bg), 0.0))
            np_.append(jnp.where(pair_diag, _dot_tn(jnp.concatenate([wmat[j], ck["vv"][:, sl]], axis=0),
                                                    jnp.concatenate([bg, kg], axis=0)), 0.0))
        for n in range(NSEQ):
            for hp in range(N_PAIRS):
                s_bd = st_s[n * N_PAIRS + hp]
                for ci, ck in enumerate(cks):
                    if ck["seq"] != n:
                        continue
                    j = ci * N_PAIRS + hp
                    y_s[ck["rows"], sls[hp]] = _dot_nt(rhat_e[j], s_bd) + yi[j]
                    s_bd = s_bd * ck["pc"][:, sls[hp]] + _dot(s_bd, mp[j]) + np_[j]
                st_s[n * N_PAIRS + hp] = s_bd
        return carry

    lax.fori_loop(0, NCH // CPI, chunk_group, 0)

    for sl in sls:
        y = y_s[:, sl]
        mean = _head_sum(y, ones2) * (1.0 / HEAD)
        d = y - mean
        var = _head_sum(d * d, ones2) * (1.0 / HEAD)
        out = d * lax.rsqrt(var + GN_EPS) * gng_ref[:, sl] + gnb_ref[:, sl] + bo_s[:, sl]
        for n in range(NSEQ):
            y_ref[n, :, sl] = out[n * TB:(n + 1) * TB]

    @pl.when(tb == pl.num_programs(1) - 1)
    def _():
        for n in range(NSEQ):
            for hp in range(N_PAIRS):
                for h in range(2):
                    sout_ref[n, 2 * hp + h] = lax.dot_general(
                        st_s[n * N_PAIRS + hp, h * HEAD:(h + 1) * HEAD, :], place[h], (((1,), (1,)), ((), ())),
                        precision=HIGHEST, preferred_element_type=F32)


def _wkv_call(p, shift0, s0, rw, *, C, TB, CPI, NSEQ=1):
    B, T, _ = p.shape
    H = 2 * N_PAIRS
    assert NSEQ == 1 or (T == TB and TB // C == CPI)
    return pl.pallas_call(
        functools.partial(_wkv_kernel, C, TB // C, CPI, NSEQ),
        grid=(B // NSEQ, T // TB),
        in_specs=[
            pl.BlockSpec((NSEQ, TB, A_SHIFT_W), lambda b, t: (b, t, 0)),
            pl.BlockSpec((NSEQ, 8, A_SHIFT_W), lambda b, t: (b, jnp.maximum(t * (TB // 8) - 1, 0), 0)),
            pl.BlockSpec((NSEQ, 1, A_SHIFT_W), lambda b, t: (b, 0, 0)),
            pl.BlockSpec((NSEQ, H, HEAD, HEAD), lambda b, t: (b, 0, 0, 0)),
        ] + [_full_spec(x.shape) for x in rw],
        out_specs=[
            pl.BlockSpec((NSEQ, TB, A_WIDTH), lambda b, t: (b, t, 0)),
            pl.BlockSpec((NSEQ, H, HEAD, HEAD), lambda b, t: (b, 0, 0, 0)),
        ],
        out_shape=[jax.ShapeDtypeStruct((B, T, A_WIDTH), F32),
                   jax.ShapeDtypeStruct((B, H, HEAD, HEAD), F32)],
        scratch_shapes=[pltpu.VMEM((NSEQ * TB, A_WIDTH), F32) for _ in range(8)]
        + [pltpu.VMEM((NSEQ * N_PAIRS, LANES, LANES), F32)],
        compiler_params=_params(("parallel", "arbitrary")),
        name="wkv7_chunked",
    )(p, p, shift0, s0, *rw)


def _even_out_kernel(ya_ref, pb_ref, pbprev_ref, u0_ref, x_ref, wout_ref, cw_ref, g_ref, b_ref,
                     o_ref, ulast_ref):
    tb = pl.program_id(1)
    TB = ya_ref.shape[1]
    pb = pb_ref[0].astype(F32)
    za, bb, cc, hh, zb = (pb[:, i * 512:(i + 1) * 512] for i in range(5))
    u = cc * hh
    n_prev = pbprev_ref.shape[1]
    tail = pbprev_ref[0].astype(F32)[n_prev - 2:n_prev]
    tail_u = tail[:, 1024:1536] * tail[:, 1536:2048]
    up2 = jnp.where(tb == 0, u0_ref[0, 6:7], tail_u[0:1])
    up1 = jnp.where(tb == 0, u0_ref[0, 7:8], tail_u[1:2])
    row = lax.broadcasted_iota(jnp.int32, (TB, 1), 0)
    u1 = jnp.where(row == 0, up1, pltpu.roll(u, 1, axis=0))
    u2 = jnp.where(row == 0, up2, jnp.where(row == 1, up1, pltpu.roll(u, 2, axis=0)))
    cw = cw_ref[...]
    yb = bb * (cw[0:1] * u2 + cw[1:2] * u1 + cw[2:3] * u)
    mix_a = ya_ref[0] * _silu(za)
    mix_b = yb * _silu(zb)
    y = _dot(mix_a, wout_ref[0:A_WIDTH, :]) + _dot(mix_b, wout_ref[A_WIDTH:, :])
    o_ref[0] = _layer_norm(DEEPNORM_ALPHA * x_ref[0] + y, g_ref[...], b_ref[...])
    ulast_ref[0] = u[TB - 8:TB]


def _even_out_call(ya, pb, u0, x, wout, cw, g, b, TB):
    B, T, _ = x.shape
    n_prev = min(16, T)
    return pl.pallas_call(
        _even_out_kernel,
        grid=(B, T // TB),
        in_specs=[
            pl.BlockSpec((1, TB, A_WIDTH), lambda b_, t: (b_, t, 0)),
            pl.BlockSpec((1, TB, EVEN_REST_W), lambda b_, t: (b_, t, 0)),
            pl.BlockSpec((1, n_prev, EVEN_REST_W), lambda b_, t: (b_, jnp.maximum(t * (TB // n_prev) - 1, 0), 0)),
            pl.BlockSpec((1, 8, B_WIDTH), lambda b_, t: (b_, 0, 0)),
            pl.BlockSpec((1, TB, D_MODEL), lambda b_, t: (b_, t, 0)),
            _full_spec(wout.shape), _full_spec(cw.shape), _full_spec(g.shape), _full_spec(b.shape),
        ],
        out_specs=[pl.BlockSpec((1, TB, D_MODEL), lambda b_, t: (b_, t, 0)),
                   pl.BlockSpec((1, 8, B_WIDTH), lambda b_, t: (b_, 0, 0))],
        out_shape=[jax.ShapeDtypeStruct((B, T, D_MODEL), F32), jax.ShapeDtypeStruct((B, 8, B_WIDTH), F32)],
        compiler_params=_params(("parallel", "arbitrary")),
        name="even_out",
    )(ya, pb, pb, u0, x, wout, cw, g, b)


def _attn_prompt_kernel(*refs):
    q_refs, k_refs, v_refs = refs[0:3], refs[3:6], refs[6:9]
    o_ref, og_s, lse_s = refs[9], refs[10], refs[11]
    BLK = ATTN_BLOCK
    lane = lax.broadcasted_iota(jnp.int32, (1, LANES), 1)
    m0 = lane < HEAD
    qi = lax.broadcasted_iota(jnp.int32, (2 * BLK, 2 * BLK), 0) % BLK
    kj = lax.broadcasted_iota(jnp.int32, (2 * BLK, 2 * BLK), 1)
    band = (kj >= qi) & (kj <= qi + BLK)
    qi1 = lax.broadcasted_iota(jnp.int32, (2 * BLK, BLK), 0) % BLK
    kj1 = lax.broadcasted_iota(jnp.int32, (2 * BLK, BLK), 1)
    causal = kj1 <= qi1
    for g, (window, d) in enumerate(C_GROUPS):
        assert window // d == BLK
        nb = q_refs[g].shape[1] // (d * BLK)
        for rho in range(d):
            k_prev = v_prev = None
            for i in range(nb):
                start = rho + d * BLK * i
                rows = pl.ds(start, BLK, stride=d) if d > 1 else pl.ds(start, BLK)
                q = q_refs[g][0, rows, :] * (HEAD ** -0.5)
                k_cur = k_refs[g][0, rows, :].astype(BF16)
                v_cur = v_refs[g][0, rows, :].astype(BF16)
                if i == 0:
                    keys, vals, mask = k_cur, v_cur, causal
                else:
                    keys = jnp.concatenate([k_prev, k_cur], axis=0)
                    vals = jnp.concatenate([v_prev, v_cur], axis=0)
                    mask = band
                q2 = jnp.concatenate([jnp.where(m0, q, 0.0), jnp.where(m0, 0.0, q)], axis=0)
                s = jnp.where(mask, _dot_nt(q2, keys), NEG_BIG)
                mx = jnp.max(s, axis=-1, keepdims=True)
                e = jnp.exp(s - mx)
                den = jnp.sum(e, axis=-1, keepdims=True)
                o2 = _dot(e, vals) * (1.0 / den)
                lse2 = mx + jnp.log(den)
                og_s[g, rows, :] = jnp.where(m0, o2[:BLK], o2[BLK:])
                lse_s[g, rows, :] = jnp.where(m0, lse2[:BLK], lse2[BLK:])
                k_prev, v_prev = k_cur, v_cur
    l0, l1, l2 = lse_s[0], lse_s[1], lse_s[2]
    mx = jnp.maximum(jnp.maximum(l0, l1), l2)
    w0, w1, w2 = jnp.exp(l0 - mx), jnp.exp(l1 - mx), jnp.exp(l2 - mx)
    o_ref[0] = (w0 * og_s[0] + w1 * og_s[1] + w2 * og_s[2]) * (1.0 / (w0 + w1 + w2))


def _attn_prompt_call(p):
    B, T, _ = p.shape
    n_g = len(C_GROUPS)

    def col_spec(part, g):
        base = (part * n_g + g) * C_WIDTH // LANES
        return pl.BlockSpec((1, T, LANES), lambda b, hp: (b, 0, base + hp))

    specs = [col_spec(part, g) for part in range(3) for g in range(n_g)]
    return pl.pallas_call(
        _attn_prompt_kernel,
        grid=(B, N_PAIRS),
        in_specs=specs,
        out_specs=pl.BlockSpec((1, T, LANES), lambda b, hp: (b, 0, hp)),
        out_shape=jax.ShapeDtypeStruct((B, T, C_WIDTH), F32),
        scratch_shapes=[pltpu.VMEM((n_g, T, LANES), F32), pltpu.VMEM((n_g, T, LANES), F32)],
        compiler_params=_params(("parallel", "parallel")),
        name="attn_prompt",
    )(*([p] * 9))


def _attn_sample_kernel(n_alias, ps_ref, *refs):
    c_refs = refs[0:3]
    o_ref = refs[3 + n_alias]
    kv_refs = refs[4 + n_alias:7 + n_alias]
    T = ps_ref.shape[1]
    p = ps_ref[0]
    lane = lax.broadcasted_iota(jnp.int32, (1, LANES), 1)
    m0 = lane < HEAD
    tq = lax.broadcasted_iota(jnp.int32, (2 * T, 1), 0) % T
    for hp in range(N_PAIRS):
        m_run = jnp.full((2 * T, 1), NEG_BIG, F32)
        l_run = jnp.zeros((2 * T, 1), F32)
        acc = jnp.zeros((2 * T, LANES), F32)
        ch = slice(hp * LANES, (hp + 1) * LANES)
        for g, (window, d) in enumerate(C_GROUPS):
            wb = c_refs[g].shape[4]
            assert wb == window and window % d == 0
            col = g * C_WIDTH + hp * LANES
            q = p[:, col:col + LANES] * (HEAD ** -0.5)
            k_new = p[:, C_QKV_W // 3 + col:C_QKV_W // 3 + col + LANES]
            v_new = p[:, 2 * C_QKV_W // 3 + col:2 * C_QKV_W // 3 + col + LANES]
            q2 = jnp.concatenate([jnp.where(m0, q, 0.0), jnp.where(m0, 0.0, q)], axis=0)
            pos = lax.broadcasted_iota(jnp.int32, (2 * T, wb), 1)
            ok_old = (pos >= tq + (wb - window)) & (((pos - tq) & (d - 1)) == 0)
            tn = lax.broadcasted_iota(jnp.int32, (2 * T, T), 1)
            ok_new = (tn <= tq) & (((tq - tn) & (d - 1)) == 0)
            s_old = jnp.where(ok_old, _dot(q2, c_refs[g][0, 0, 0, ch, :]), NEG_BIG)
            s_new = jnp.where(ok_new, _dot_nt(q2, k_new), NEG_BIG)
            m_new = jnp.maximum(m_run, jnp.maximum(jnp.max(s_old, axis=-1, keepdims=True),
                                                   jnp.max(s_new, axis=-1, keepdims=True)))
            e_old = jnp.exp(s_old - m_new)
            e_new = jnp.exp(s_new - m_new)
            scale = jnp.exp(m_run - m_new)
            l_run = l_run * scale + jnp.sum(e_old, axis=-1, keepdims=True) + jnp.sum(e_new, axis=-1, keepdims=True)
            acc = acc * scale + _dot_nt(e_old, c_refs[g][0, 0, 1, ch, :]) + _dot(e_new, v_new)
            m_run = m_new
        o2 = acc * (1.0 / l_run)
        o_ref[0, :, ch] = jnp.where(m0, o2[:T], o2[T:])
    tail = lane >= LANES - T
    for g in range(len(C_GROUPS)):
        wb = c_refs[g].shape[4]
        for part in range(2):
            c0 = (part + 1) * C_QKV_W // 3 + g * C_WIDTH
            new_rows = jnp.concatenate([jnp.zeros((LANES - T, C_WIDTH), F32), p[:, c0:c0 + C_WIDTH]], axis=0)
            new_t = new_rows.T
            for rb in range(C_WIDTH // LANES):
                rs = slice(rb * LANES, (rb + 1) * LANES)
                rolled = pltpu.roll(c_refs[g][0, 0, part, rs, :], wb - T, axis=1)
                if wb > LANES:
                    kv_refs[g][0, 0, part, rs, 0:wb - LANES] = rolled[:, 0:wb - LANES]
                kv_refs[g][0, 0, part, rs, wb - LANES:wb] = jnp.where(tail, new_t[rs], rolled[:, wb - LANES:wb])


def _attn_sample_call(p, caches, layer_slot, kv_prev):
    B, T, _ = p.shape
    cache_spec = lambda c: pl.BlockSpec((1, 1) + c.shape[2:], lambda b: (layer_slot, b, 0, 0, 0))
    in_specs = [pl.BlockSpec((1, T, ODD_IN_W), lambda b: (b, 0, 0))] + [cache_spec(c) for c in caches]
    args = [p] + list(caches)
    aliases = {}
    n_alias = 0
    if kv_prev is not None:
        n_alias = 3
        in_specs += [pl.BlockSpec(memory_space=pl.ANY)] * 3
        args += list(kv_prev)
        aliases = {4: 1, 5: 2, 6: 3}
    outs = pl.pallas_call(
        functools.partial(_attn_sample_kernel, n_alias),
        grid=(B,),
        in_specs=in_specs,
        out_specs=[pl.BlockSpec((1, T, C_WIDTH), lambda b: (b, 0, 0))] + [cache_spec(c) for c in caches],
        out_shape=[jax.ShapeDtypeStruct((B, T, C_WIDTH), F32)]
        + [jax.ShapeDtypeStruct(c.shape, F32) for c in caches],
        input_output_aliases=aliases,
        compiler_params=_params(("parallel",)),
        name="attn_sample",
    )(*args)
    return outs[0], outs[1:]


def _odd_out_kernel(o_ref, z_ref, x_ref, wout_ref, g_ref, b_ref, out_ref):
    y = _dot(o_ref[...] * _silu(z_ref[...]), wout_ref[...])
    out_ref[...] = _layer_norm(DEEPNORM_ALPHA * x_ref[...] + y, g_ref[...], b_ref[...])


def _odd_out_call(o, p, x, wout, g, b, tm):
    B, T, D = x.shape
    M = B * T
    out = pl.pallas_call(
        _odd_out_kernel,
        grid=(M // tm,),
        in_specs=[
            pl.BlockSpec((tm, C_WIDTH), lambda i: (i, 0)),
            pl.BlockSpec((tm, C_WIDTH), lambda i: (i, C_QKV_W // C_WIDTH)),
            pl.BlockSpec((tm, D), lambda i: (i, 0)),
            _full_spec(wout.shape), _full_spec(g.shape), _full_spec(b.shape),
        ],
        out_specs=pl.BlockSpec((tm, D), lambda i: (i, 0)),
        out_shape=jax.ShapeDtypeStruct((M, D), F32),
        compiler_params=_params(("parallel",)),
        name="odd_out",
    )(o.reshape(M, C_WIDTH), p.reshape(M, ODD_IN_W), x.reshape(M, D), wout, g, b)
    return out.reshape(B, T, D)


def kernel(x_prompt, x_sample, state_rwkv, state_shift, state_conv, cache_kv_w128, cache_kv_w512, cache_kv_w2048, even_w_in, even_w_out, rwkv_mu, rwkv_w0, rwkv_w2, rwkv_a0, rwkv_a2, rwkv_k_k, rwkv_k_a, rwkv_r_k, rwkv_gn_g, rwkv_gn_b, conv_w, odd_w_in, odd_w_out, ln_g, ln_b):
    xp, xs = x_prompt, x_sample
    BP, TP, _ = xp.shape
    BS, TS, _ = xs.shape
    row = lambda t: t.reshape(1, -1)
    to_channel_major = lambda c: jnp.transpose(c, (0, 1, 3, 4, 5, 2)).reshape(
        c.shape[0], c.shape[1], 2, C_WIDTH, c.shape[2])
    caches = [to_channel_major(c) for c in (cache_kv_w128, cache_kv_w512, cache_kv_w2048)]
    rwkv_p, rwkv_s, shift_p, shift_s, conv_p, conv_s = [], [], [], [], [], []
    kv_p = kv_s = None
    for li in range(DEPTH):
        j = li // 2
        g, b = row(ln_g[li]), row(ln_b[li])
        if li % 2 == 0:
            w_in = even_w_in[j].astype(BF16)
            w_a, w_rest = w_in[:, :A_SHIFT_W], w_in[:, A_SHIFT_W:]
            w_out = even_w_out[j].astype(BF16)
            zeros_lora = jnp.zeros((HEAD, A_WIDTH), F32)
            rw = (row(rwkv_mu[j]), row(rwkv_w0[j]), _stack_split(jnp.concatenate([rwkv_w2[j], zeros_lora], 0)),
                  row(rwkv_a0[j]), _stack_split(jnp.concatenate([zeros_lora, rwkv_a2[j]], 0)), row(rwkv_k_k[j]),
                  row(rwkv_k_a[j]), row(rwkv_r_k[j]), row(rwkv_gn_g[j]), row(rwkv_gn_b[j]))
            pa, pb = _proj_call(xp, [w_a, w_rest], tm=512, out_dtypes=[F32, BF16])
            ya, s_new = _wkv_call(pa, jnp.zeros((BP, 1, A_SHIFT_W), F32),
                                  jnp.zeros((BP, 2 * N_PAIRS, HEAD, HEAD), F32), rw,
                                  C=WKV_CHUNK, TB=8 * WKV_CHUNK, CPI=4)
            xp, ulast = _even_out_call(ya, pb, jnp.zeros((BP, 8, B_WIDTH), F32), xp, w_out, conv_w[j], g, b,
                                       TB=4 * WKV_CHUNK)
            rwkv_p.append(s_new)
            shift_p.append(pa[:, -1, :])
            conv_p.append(ulast[:, 6:8, :])
            pa, pb = _proj_call(xs, [w_a, w_rest], tm=BS * TS, out_dtypes=[F32, BF16])
            ya, s_new = _wkv_call(pa, state_shift[j].reshape(BS, 1, A_SHIFT_W), state_rwkv[j], rw,
                                  C=TS, TB=TS, CPI=1, NSEQ=4)
            u0 = jnp.concatenate([jnp.zeros((BS, 6, B_WIDTH), F32), state_conv[j]], axis=1)
            xs, ulast = _even_out_call(ya, pb, u0, xs, w_out, conv_w[j], g, b, TB=TS)
            rwkv_s.append(s_new)
            shift_s.append(pa[:, -1, :])
            conv_s.append(ulast[:, 6:8, :])
        else:
            w_in = odd_w_in[j].astype(BF16)
            w_out = odd_w_out[j].astype(BF16)
            pp, kv_p = _proj_odd_prompt_call(xp, w_in, j, kv_p)
            op = _attn_prompt_call(pp)
            xp = _odd_out_call(op, pp, xp, w_out, g, b, tm=512)
            (ps,) = _proj_call(xs, [w_in], tm=BS * TS)
            os_, kv_s = _attn_sample_call(ps, caches, j, kv_s)
            xs = _odd_out_call(os_, ps, xs, w_out, g, b, tm=BS * TS)
    kv_shape = lambda t: jnp.transpose(
        t.reshape(t.shape[0], t.shape[1], 2, C_WIDTH // HEAD, HEAD, t.shape[4]), (0, 1, 5, 2, 3, 4))
    return (xp, xs,
            jnp.stack(rwkv_p), jnp.stack(rwkv_s),
            jnp.stack(shift_p), jnp.stack(shift_s),
            jnp.stack(conv_p), jnp.stack(conv_s),
            kv_shape(kv_p[0]), kv_shape(kv_s[0]),
            kv_shape(kv_p[1]), kv_shape(kv_s[1]),
            kv_shape(kv_p[2]), kv_shape(kv_s[2]))
```

```python
import functools

import jax
import jax.numpy as jnp
from jax import lax
from jax.experimental import pallas as pl
from jax.experimental.pallas import tpu as pltpu

F32 = jnp.float32
BF16 = jnp.bfloat16
HIGHEST = lax.Precision.HIGHEST

D_MODEL = 1024
DEPTH = 4
HEAD = 64
LANES = 128
A_WIDTH = 512
N_PAIRS = A_WIDTH // LANES
A_SHIFT_W = 3 * A_WIDTH + 2 * HEAD
B_WIDTH = 512
EVEN_REST_W = A_WIDTH + 4 * B_WIDTH
C_GROUPS = ((128, 1), (512, 4), (2048, 16))
C_WIDTH = 512
C_QKV_W = 3 * len(C_GROUPS) * C_WIDTH
ODD_IN_W = C_QKV_W + C_WIDTH
ATTN_BLOCK = 128
GN_EPS = 64e-5
LN_EPS = 1e-5
DEEPNORM_ALPHA = (2 * DEPTH) ** 0.25
NEG_BIG = -1e30
WKV_CHUNK = 64
VMEM_LIMIT = 56 * 1024 * 1024


def _dot(a, b):
    return jnp.dot(a.astype(BF16), b.astype(BF16), preferred_element_type=F32)


def _dot_nt(a, b):
    return lax.dot_general(a.astype(BF16), b.astype(BF16), (((1,), (1,)), ((), ())),
                           preferred_element_type=F32)


def _dot_tn(a, b):
    return lax.dot_general(a.astype(BF16), b.astype(BF16), (((0,), (0,)), ((), ())),
                           preferred_element_type=F32)


def _dot_f32(a, b):
    return jnp.dot(a, b, precision=HIGHEST, preferred_element_type=F32)


def _split_bf16(x):
    hi = x.astype(BF16)
    return hi, (x - hi.astype(F32)).astype(BF16)


def _dot_split(a, b3):
    hi, lo = _split_bf16(a)
    return jnp.dot(jnp.concatenate([hi, hi, lo], axis=1), b3, preferred_element_type=F32)


def _stack_split(b):
    hi, lo = _split_bf16(b)
    return jnp.concatenate([hi, lo, hi], axis=0)


def _silu(z):
    return z * (1.0 / (1.0 + jnp.exp(-z)))


def _layer_norm(xf, g, b):
    mu = jnp.mean(xf, axis=-1, keepdims=True)
    d = xf - mu
    var = jnp.mean(d * d, axis=-1, keepdims=True)
    return d * lax.rsqrt(var + LN_EPS) * g + b


def _full_spec(shape):
    return pl.BlockSpec(shape, lambda *_: (0,) * len(shape))


def _params(semantics):
    return pltpu.CompilerParams(dimension_semantics=semantics, vmem_limit_bytes=VMEM_LIMIT)


def _proj_kernel(n_out, x_ref, *refs):
    w_refs, o_refs = refs[:n_out], refs[n_out:]
    xb = x_ref[...].astype(BF16)
    for w_ref, o_ref in zip(w_refs, o_refs):
        o_ref[...] = jnp.dot(xb, w_ref[...], preferred_element_type=F32).astype(o_ref.dtype)


def _proj_call(x, weights, tm, out_dtypes=None):
    B, T, D = x.shape
    M = B * T
    out_dtypes = out_dtypes or [F32] * len(weights)
    outs = pl.pallas_call(
        functools.partial(_proj_kernel, len(weights)),
        grid=(M // tm,),
        in_specs=[pl.BlockSpec((tm, D), lambda i: (i, 0))] + [_full_spec(w.shape) for w in weights],
        out_specs=[pl.BlockSpec((tm, w.shape[1]), lambda i: (i, 0)) for w in weights],
        out_shape=[jax.ShapeDtypeStruct((M, w.shape[1]), dt) for w, dt in zip(weights, out_dtypes)],
        compiler_params=_params(("parallel",)),
        name="in_proj",
    )(x.reshape(M, D), *weights)
    return [o.reshape(B, T, -1) for o in outs]


def _kept_rows(window, T, tm):
    rows = min(window, tm)
    return rows, min(window, T) // rows


def _proj_odd_prompt_kernel(tm, T, x_ref, w_ref, *refs):
    p_ref, kv_refs = refs[-4], refs[-3:]
    i = pl.program_id(1)
    nt = T // tm
    p_ref[0] = jnp.dot(x_ref[0].astype(BF16), w_ref[...], preferred_element_type=F32)
    for g, ((window, _), kv_ref) in enumerate(zip(C_GROUPS, kv_refs)):
        rows, kt = _kept_rows(window, T, tm)

        @pl.when(i >= nt - kt)
        def _():
            for part in range(2):
                c0 = (part + 1) * C_QKV_W // 3 + g * C_WIDTH
                kv_ref[0, 0, part] = p_ref[0, tm - rows:tm, c0:c0 + C_WIDTH].T


def _proj_odd_prompt_call(x, w, layer_slot, kv_prev, tm=512):
    B, T, D = x.shape
    nt = T // tm
    n_odd = DEPTH // 2
    kv_specs, kv_shapes = [], []
    for window, _ in C_GROUPS:
        rows, kt = _kept_rows(window, T, tm)

        def index_map(b, i, kt=kt):
            return (layer_slot, b, 0, 0, jnp.maximum(i - (nt - kt), 0))

        kv_specs.append(pl.BlockSpec((1, 1, 2, C_WIDTH, rows), index_map))
        kv_shapes.append(jax.ShapeDtypeStruct((n_odd, B, 2, C_WIDTH, rows * kt), F32))
    in_specs = [pl.BlockSpec((1, tm, D), lambda b, i: (b, i, 0)), _full_spec(w.shape)]
    args = [x, w]
    aliases = {}
    if kv_prev is not None:
        in_specs += [pl.BlockSpec(memory_space=pl.ANY)] * 3
        args += list(kv_prev)
        aliases = {2: 1, 3: 2, 4: 3}
    outs = pl.pallas_call(
        functools.partial(_proj_odd_prompt_kernel, tm, T),
        grid=(B, nt),
        in_specs=in_specs,
        out_specs=[pl.BlockSpec((1, tm, ODD_IN_W), lambda b, i: (b, i, 0))] + kv_specs,
        out_shape=[jax.ShapeDtypeStruct((B, T, ODD_IN_W), F32)] + kv_shapes,
        input_output_aliases=aliases,
        compiler_params=_params(("parallel", "arbitrary")),
        name="in_proj_attn_prompt",
    )(*args)
    return outs[0], outs[1:]


def _head_sum(x, ones2):
    xh = x.astype(BF16)
    xl = (x - xh.astype(F32)).astype(BF16)
    return jnp.dot(jnp.concatenate([xh, xl], axis=1), ones2, preferred_element_type=F32)


def _prefix_sum(x, n):
    row = lax.broadcasted_iota(jnp.int32, (n, 1), 0)
    s = 1
    while s < n:
        x = x + jnp.where(row >= s, pltpu.roll(x, s, axis=0), 0.0)
        s *= 2
    return x


def _wkv_kernel(C, NCH, CPI, NSEQ, p_ref, pprev_ref, shift0_ref, s0_ref, mu_ref, w0_ref, w2p_ref, a0_ref, a2p_ref,
                kk_ref, ka_ref, rk_ref, gng_ref, gnb_ref,
                y_ref, sout_ref,
                r_s, k_s, v_s, al_s, be_s, lw_s, bo_s, y_s, st_s):
    tb = pl.program_id(1)
    TB = C * NCH
    ROWS = NSEQ * TB

    ei = lax.broadcasted_iota(jnp.int32, (HEAD, LANES), 0)
    ej = lax.broadcasted_iota(jnp.int32, (HEAD, LANES), 1)
    place = [jnp.where(ej == ei + h * HEAD, 1.0, 0.0).astype(F32) for h in range(2)]

    @pl.when(tb == 0)
    def _():
        for n in range(NSEQ):
            for hp in range(N_PAIRS):
                for h in range(2):
                    st_s[n * N_PAIRS + hp, h * HEAD:(h + 1) * HEAD, :] = _dot_f32(s0_ref[n, 2 * hp + h], place[h])

    p = jnp.concatenate([p_ref[n] for n in range(NSEQ)], axis=0)
    first_rows = [jnp.where(tb == 0, shift0_ref[n], pprev_ref[n, 7:8, :]) for n in range(NSEQ)]
    first_rows = first_rows[0] if NSEQ == 1 else jnp.concatenate(
        [jnp.broadcast_to(f, (TB, A_SHIFT_W)) for f in first_rows], axis=0)
    row = lax.broadcasted_iota(jnp.int32, (ROWS, 1), 0)
    prev = jnp.where(row % TB == 0, first_rows, pltpu.roll(p, 1, axis=0))
    ps = p + (prev - p) * mu_ref[...]
    r = ps[:, 0:A_WIDTH]
    k = ps[:, A_WIDTH:2 * A_WIDTH]
    v = ps[:, 2 * A_WIDTH:3 * A_WIDTH]
    lora_in = ps[:, 3 * A_WIDTH:A_SHIFT_W]
    nw = -(w0_ref[...] + _dot_split(jnp.tanh(lora_in), w2p_ref[...]))
    softplus = jnp.maximum(nw, 0.0) + jnp.log(1.0 + jnp.exp(-jnp.abs(nw)))
    lw = -jnp.exp(-softplus - 0.5)
    a = 1.0 / (1.0 + jnp.exp(-(a0_ref[...] + _dot_split(lora_in, a2p_ref[...]))))

    li = lax.broadcasted_iota(jnp.int32, (2 * LANES, LANES), 0)
    lj = lax.broadcasted_iota(jnp.int32, (2 * LANES, LANES), 1)
    ones2 = jnp.where(((li % LANES) // HEAD) == (lj // HEAD), 1.0, 0.0).astype(BF16)

    kk = k * kk_ref[...]
    kmod = k * (1.0 + (a - 1.0) * ka_ref[...])
    rk = r * kmod * rk_ref[...]
    sls = [slice(hp * LANES, (hp + 1) * LANES) for hp in range(N_PAIRS)]
    for sl in sls:
        kkp = kk[:, sl]
        kkn = kkp * lax.rsqrt(jnp.maximum(_head_sum(kkp * kkp, ones2), 1e-24))
        al_s[:, sl] = -kkn
        be_s[:, sl] = kkn * a[:, sl]
        bo_s[:, sl] = _head_sum(rk[:, sl], ones2) * v[:, sl]
    r_s[...] = r
    k_s[...] = kmod
    v_s[...] = v
    lw_s[...] = lw

    ti = lax.broadcasted_iota(jnp.int32, (C, C), 0)
    tj = lax.broadcasted_iota(jnp.int32, (C, C), 1)
    strict = ti > tj
    incl = ti >= tj
    eye = jnp.where(ti == tj, 1.0, 0.0).astype(F32)
    level_masks = []
    b = 1
    while b < C:
        level_masks.append(((ti // (2 * b)) == (tj // (2 * b))) & (((ti // b) % 2) == 1) & (((tj // b) % 2) == 0))
        b *= 2
    lane = lax.broadcasted_iota(jnp.int32, (1, LANES), 1)
    head_masks = (lane < HEAD, lane >= HEAD)
    m0 = head_masks[0]
    bi = lax.broadcasted_iota(jnp.int32, (LANES, LANES), 0)
    bj = lax.broadcasted_iota(jnp.int32, (LANES, LANES), 1)
    pair_diag = (bi // HEAD) == (bj // HEAD)

    def chunk_group(it, carry):
        cks = []
        for n, ci in [(n, ci) for n in range(NSEQ) for ci in range(CPI)]:
            rows = pl.ds(pl.multiple_of(n * TB + (it * CPI + ci) * C, C), C)
            lw_c = lw_s[rows, :]
            linc = _prefix_sum(lw_c, C)
            lm = linc[C // 2 - 1:C // 2, :]
            lend = linc[C - 1:C, :]
            e_fwd = jnp.exp(linc - lm)
            e_bwd = jnp.exp(lm - linc)
            cks.append(dict(
                rows=rows, seq=n,
                at=al_s[rows, :] * jnp.exp(linc - lw_c - lm),
                rt=r_s[rows, :] * e_fwd,
                bt=be_s[rows, :] * e_bwd,
                kt=k_s[rows, :] * e_bwd,
                vv=v_s[rows, :],
                em=jnp.exp(lm), g=jnp.exp(lend - lm), pc=jnp.exp(lend)))
        units = [(ck, hp, h) for ck in cks for hp in range(N_PAIRS) for h in range(2)]
        pairs = [(ck, hp) for ck in cks for hp in range(N_PAIRS)]
        gb, gk = [], []
        for ck, hp, h in units:
            x = jnp.concatenate([jnp.where(head_masks[h], ck["at"][:, sls[hp]], 0.0),
                                 jnp.where(head_masks[h], ck["rt"][:, sls[hp]], 0.0)], axis=0)
            gb.append(_dot_nt(x, ck["bt"][:, sls[hp]]))
            gk.append(_dot_nt(x, ck["kt"][:, sls[hp]]))
        aab = [jnp.where(strict, m[:C], 0.0) for m in gb]
        aak = [jnp.where(strict, m[:C], 0.0) for m in gk]
        arb = [jnp.where(incl, m[C:], 0.0) for m in gb]
        ark = [jnp.where(incl, m[C:], 0.0) for m in gk]
        tinv = [eye + jnp.where(level_masks[0], n, 0.0) for n in aab]
        for lmask in level_masks[1:]:
            tmp = [_dot(t, jnp.where(lmask, n, 0.0)) for t, n in zip(tinv, aab)]
            tinv = [t + _dot(x, t) for t, x in zip(tinv, tmp)]
        av = [_dot(aak[i], ck["vv"][:, sls[hp]]) for i, (ck, hp, h) in enumerate(units)]
        zz = [jnp.concatenate([ck["at"][:, sls[hp]], jnp.where(m0, av[2 * j], av[2 * j + 1])], axis=1)
              for j, (ck, hp) in enumerate(pairs)]
        tz = [_dot(tinv[i], zz[i // 2]) for i in range(len(units))]
        ahat = [jnp.where(m0, tz[2 * j][:, :LANES], tz[2 * j + 1][:, :LANES]) for j in range(len(pairs))]
        wmat = [jnp.where(m0, tz[2 * j][:, LANES:], tz[2 * j + 1][:, LANES:]) for j in range(len(pairs))]
        aw = [jnp.concatenate([ahat[j], wmat[j]], axis=1) for j in range(len(pairs))]
        ry = [_dot(arb[i], aw[i // 2]) for i in range(len(units))]
        akv = [_dot(ark[i], ck["vv"][:, sls[hp]]) for i, (ck, hp, h) in enumerate(units)]
        rhat_e, yi, mp, np_ = [], [], [], []
        for j, (ck, hp) in enumerate(pairs):
            sl = sls[hp]
            em_p, g_p = ck["em"][:, sl], ck["g"][:, sl]
            bg = ck["bt"][:, sl] * g_p
            kg = ck["kt"][:, sl] * g_p
            rhat_e.append((ck["rt"][:, sl] + jnp.where(m0, ry[2 * j][:, :LANES], ry[2 * j + 1][:, :LANES])) * em_p)
            yi.append(jnp.where(m0, ry[2 * j][:, LANES:] + akv[2 * j], ry[2 * j + 1][:, LANES:] + akv[2 * j + 1]))
            mp.append(jnp.where(pair_diag, _dot_tn(ahat[j] * em_p, bg), 0.0))
            np_.append(jnp.where(pair_diag, _dot_tn(jnp.concatenate([wmat[j], ck["vv"][:, sl]], axis=0),
                                                    jnp.concatenate([bg, kg], axis=0)), 0.0))
        for n in range(NSEQ):
            for hp in range(N_PAIRS):
                s_bd = st_s[n * N_PAIRS + hp]
                for ci, ck in enumerate(cks):
                    if ck["seq"] != n:
                        continue
                    j = ci * N_PAIRS + hp
                    y_s[ck["rows"], sls[hp]] = _dot_nt(rhat_e[j], s_bd) + yi[j]
                    s_bd = s_bd * ck["pc"][:, sls[hp]] + _dot(s_bd, mp[j]) + np_[j]
                st_s[n * N_PAIRS + hp] = s_bd
        return carry

    lax.fori_loop(0, NCH // CPI, chunk_group, 0)

    for sl in sls:
        y = y_s[:, sl]
        mean = _head_sum(y, ones2) * (1.0 / HEAD)
        d = y - mean
        var = _head_sum(d * d, ones2) * (1.0 / HEAD)
        out = d * lax.rsqrt(var + GN_EPS) * gng_ref[:, sl] + gnb_ref[:, sl] + bo_s[:, sl]
        for n in range(NSEQ):
            y_ref[n, :, sl] = out[n * TB:(n + 1) * TB].astype(y_ref.dtype)

    @pl.when(tb == pl.num_programs(1) - 1)
    def _():
        for n in range(NSEQ):
            for hp in range(N_PAIRS):
                for h in range(2):
                    sout_ref[n, 2 * hp + h] = lax.dot_general(
                        st_s[n * N_PAIRS + hp, h * HEAD:(h + 1) * HEAD, :], place[h], (((1,), (1,)), ((), ())),
                        precision=HIGHEST, preferred_element_type=F32)


def _wkv_call(p, shift0, s0, rw, *, C, TB, CPI, NSEQ=1):
    B, T, _ = p.shape
    H = 2 * N_PAIRS
    assert NSEQ == 1 or (T == TB and TB // C == CPI)
    return pl.pallas_call(
        functools.partial(_wkv_kernel, C, TB // C, CPI, NSEQ),
        grid=(B // NSEQ, T // TB),
        in_specs=[
            pl.BlockSpec((NSEQ, TB, A_SHIFT_W), lambda b, t: (b, t, 0)),
            pl.BlockSpec((NSEQ, 8, A_SHIFT_W), lambda b, t: (b, jnp.maximum(t * (TB // 8) - 1, 0), 0)),
            pl.BlockSpec((NSEQ, 1, A_SHIFT_W), lambda b, t: (b, 0, 0)),
            pl.BlockSpec((NSEQ, H, HEAD, HEAD), lambda b, t: (b, 0, 0, 0)),
        ] + [_full_spec(x.shape) for x in rw],
        out_specs=[
            pl.BlockSpec((NSEQ, TB, A_WIDTH), lambda b, t: (b, t, 0)),
            pl.BlockSpec((NSEQ, H, HEAD, HEAD), lambda b, t: (b, 0, 0, 0)),
        ],
        out_shape=[jax.ShapeDtypeStruct((B, T, A_WIDTH), BF16),
                   jax.ShapeDtypeStruct((B, H, HEAD, HEAD), F32)],
        scratch_shapes=[pltpu.VMEM((NSEQ * TB, A_WIDTH), F32) for _ in range(8)]
        + [pltpu.VMEM((NSEQ * N_PAIRS, LANES, LANES), F32)],
        compiler_params=_params(("parallel", "arbitrary")),
        name="wkv7_chunked",
    )(p, p, shift0, s0, *rw)


def _even_out_kernel(ya_ref, pb_ref, pbprev_ref, u0_ref, x_ref, wout_ref, cw_ref, g_ref, b_ref,
                     o_ref, ulast_ref):
    tb = pl.program_id(1)
    TB = ya_ref.shape[1]
    pb = pb_ref[0].astype(F32)
    za, bb, cc, hh, zb = (pb[:, i * 512:(i + 1) * 512] for i in range(5))
    u = cc * hh
    n_prev = pbprev_ref.shape[1]
    tail = pbprev_ref[0].astype(F32)[n_prev - 2:n_prev]
    tail_u = tail[:, 1024:1536] * tail[:, 1536:2048]
    up2 = jnp.where(tb == 0, u0_ref[0, 6:7], tail_u[0:1])
    up1 = jnp.where(tb == 0, u0_ref[0, 7:8], tail_u[1:2])
    row = lax.broadcasted_iota(jnp.int32, (TB, 1), 0)
    u1 = jnp.where(row == 0, up1, pltpu.roll(u, 1, axis=0))
    u2 = jnp.where(row == 0, up2, jnp.where(row == 1, up1, pltpu.roll(u, 2, axis=0)))
    cw = cw_ref[...]
    yb = bb * (cw[0:1] * u2 + cw[1:2] * u1 + cw[2:3] * u)
    mix_a = ya_ref[0].astype(F32) * _silu(za)
    mix_b = yb * _silu(zb)
    y = _dot(mix_a, wout_ref[0:A_WIDTH, :]) + _dot(mix_b, wout_ref[A_WIDTH:, :])
    o_ref[0] = _layer_norm(DEEPNORM_ALPHA * x_ref[0] + y, g_ref[...], b_ref[...])
    ulast_ref[0] = u[TB - 8:TB]


def _even_out_call(ya, pb, u0, x, wout, cw, g, b, TB):
    B, T, _ = x.shape
    n_prev = min(16, T)
    return pl.pallas_call(
        _even_out_kernel,
        grid=(B, T // TB),
        in_specs=[
            pl.BlockSpec((1, TB, A_WIDTH), lambda b_, t: (b_, t, 0)),
            pl.BlockSpec((1, TB, EVEN_REST_W), lambda b_, t: (b_, t, 0)),
            pl.BlockSpec((1, n_prev, EVEN_REST_W), lambda b_, t: (b_, jnp.maximum(t * (TB // n_prev) - 1, 0), 0)),
            pl.BlockSpec((1, 8, B_WIDTH), lambda b_, t: (b_, 0, 0)),
            pl.BlockSpec((1, TB, D_MODEL), lambda b_, t: (b_, t, 0)),
            _full_spec(wout.shape), _full_spec(cw.shape), _full_spec(g.shape), _full_spec(b.shape),
        ],
        out_specs=[pl.BlockSpec((1, TB, D_MODEL), lambda b_, t: (b_, t, 0)),
                   pl.BlockSpec((1, 8, B_WIDTH), lambda b_, t: (b_, 0, 0))],
        out_shape=[jax.ShapeDtypeStruct((B, T, D_MODEL), F32), jax.ShapeDtypeStruct((B, 8, B_WIDTH), F32)],
        compiler_params=_params(("parallel", "arbitrary")),
        name="even_out",
    )(ya, pb, pb, u0, x, wout, cw, g, b)


def _attn_prompt_kernel(*refs):
    q_refs, k_refs, v_refs = refs[0:3], refs[3:6], refs[6:9]
    o_ref, og_s, lse_s = refs[9], refs[10], refs[11]
    BLK = ATTN_BLOCK
    lane = lax.broadcasted_iota(jnp.int32, (1, LANES), 1)
    m0 = lane < HEAD
    qi = lax.broadcasted_iota(jnp.int32, (2 * BLK, 2 * BLK), 0) % BLK
    kj = lax.broadcasted_iota(jnp.int32, (2 * BLK, 2 * BLK), 1)
    band = (kj >= qi) & (kj <= qi + BLK)
    qi1 = lax.broadcasted_iota(jnp.int32, (2 * BLK, BLK), 0) % BLK
    kj1 = lax.broadcasted_iota(jnp.int32, (2 * BLK, BLK), 1)
    causal = kj1 <= qi1
    for g, (window, d) in enumerate(C_GROUPS):
        assert window // d == BLK
        nb = q_refs[g].shape[1] // (d * BLK)
        for rho in range(d):
            k_prev = v_prev = None
            for i in range(nb):
                start = rho + d * BLK * i
                rows = pl.ds(start, BLK, stride=d) if d > 1 else pl.ds(start, BLK)
                q = q_refs[g][0, rows, :] * (HEAD ** -0.5)
                k_cur = k_refs[g][0, rows, :].astype(BF16)
                v_cur = v_refs[g][0, rows, :].astype(BF16)
                if i == 0:
                    keys, vals, mask = k_cur, v_cur, causal
                else:
                    keys = jnp.concatenate([k_prev, k_cur], axis=0)
                    vals = jnp.concatenate([v_prev, v_cur], axis=0)
                    mask = band
                q2 = jnp.concatenate([jnp.where(m0, q, 0.0), jnp.where(m0, 0.0, q)], axis=0)
                s = jnp.where(mask, _dot_nt(q2, keys), NEG_BIG)
                mx = jnp.max(s, axis=-1, keepdims=True)
                e = jnp.exp(s - mx)
                den = jnp.sum(e, axis=-1, keepdims=True)
                o2 = _dot(e, vals) * (1.0 / den)
                lse2 = mx + jnp.log(den)
                og_s[g, rows, :] = jnp.where(m0, o2[:BLK], o2[BLK:])
                lse_s[g, rows, :] = jnp.where(m0, lse2[:BLK], lse2[BLK:])
                k_prev, v_prev = k_cur, v_cur
    l0, l1, l2 = lse_s[0], lse_s[1], lse_s[2]
    mx = jnp.maximum(jnp.maximum(l0, l1), l2)
    w0, w1, w2 = jnp.exp(l0 - mx), jnp.exp(l1 - mx), jnp.exp(l2 - mx)
    o_ref[0] = (w0 * og_s[0] + w1 * og_s[1] + w2 * og_s[2]) * (1.0 / (w0 + w1 + w2))


def _attn_prompt_call(p):
    B, T, _ = p.shape
    n_g = len(C_GROUPS)

    def col_spec(part, g):
        base = (part * n_g + g) * C_WIDTH // LANES
        return pl.BlockSpec((1, T, LANES), lambda b, hp: (b, 0, base + hp))

    specs = [col_spec(part, g) for part in range(3) for g in range(n_g)]
    return pl.pallas_call(
        _attn_prompt_kernel,
        grid=(B, N_PAIRS),
        in_specs=specs,
        out_specs=pl.BlockSpec((1, T, LANES), lambda b, hp: (b, 0, hp)),
        out_shape=jax.ShapeDtypeStruct((B, T, C_WIDTH), F32),
        scratch_shapes=[pltpu.VMEM((n_g, T, LANES), F32), pltpu.VMEM((n_g, T, LANES), F32)],
        compiler_params=_params(("parallel", "parallel")),
        name="attn_prompt",
    )(*([p] * 9))


def _attn_sample_kernel(n_alias, ps_ref, *refs):
    c_refs = refs[0:3]
    o_ref = refs[3 + n_alias]
    kv_refs = refs[4 + n_alias:7 + n_alias]
    T = ps_ref.shape[1]
    p = ps_ref[0]
    lane = lax.broadcasted_iota(jnp.int32, (1, LANES), 1)
    m0 = lane < HEAD
    tq = lax.broadcasted_iota(jnp.int32, (2 * T, 1), 0) % T
    for hp in range(N_PAIRS):
        m_run = jnp.full((2 * T, 1), NEG_BIG, F32)
        l_run = jnp.zeros((2 * T, 1), F32)
        acc = jnp.zeros((2 * T, LANES), F32)
        ch = slice(hp * LANES, (hp + 1) * LANES)
        for g, (window, d) in enumerate(C_GROUPS):
            wb = c_refs[g].shape[4]
            assert wb == window and window % d == 0
            col = g * C_WIDTH + hp * LANES
            q = p[:, col:col + LANES] * (HEAD ** -0.5)
            k_new = p[:, C_QKV_W // 3 + col:C_QKV_W // 3 + col + LANES]
            v_new = p[:, 2 * C_QKV_W // 3 + col:2 * C_QKV_W // 3 + col + LANES]
            q2 = jnp.concatenate([jnp.where(m0, q, 0.0), jnp.where(m0, 0.0, q)], axis=0)
            pos = lax.broadcasted_iota(jnp.int32, (2 * T, wb), 1)
            ok_old = (pos >= tq + (wb - window)) & (((pos - tq) & (d - 1)) == 0)
            tn = lax.broadcasted_iota(jnp.int32, (2 * T, T), 1)
            ok_new = (tn <= tq) & (((tq - tn) & (d - 1)) == 0)
            s_old = jnp.where(ok_old, _dot(q2, c_refs[g][0, 0, 0, ch, :]), NEG_BIG)
            s_new = jnp.where(ok_new, _dot_nt(q2, k_new), NEG_BIG)
            m_new = jnp.maximum(m_run, jnp.maximum(jnp.max(s_old, axis=-1, keepdims=True),
                                                   jnp.max(s_new, axis=-1, keepdims=True)))
            e_old = jnp.exp(s_old - m_new)
            e_new = jnp.exp(s_new - m_new)
            scale = jnp.exp(m_run - m_new)
            l_run = l_run * scale + jnp.sum(e_old, axis=-1, keepdims=True) + jnp.sum(e_new, axis=-1, keepdims=True)
            acc = acc * scale + _dot_nt(e_old, c_refs[g][0, 0, 1, ch, :]) + _dot(e_new, v_new)
            m_run = m_new
        o2 = acc * (1.0 / l_run)
        o_ref[0, :, ch] = jnp.where(m0, o2[:T], o2[T:])
    tail = lane >= LANES - T
    for g in range(len(C_GROUPS)):
        wb = c_refs[g].shape[4]
        for part in range(2):
            c0 = (part + 1) * C_QKV_W // 3 + g * C_WIDTH
            new_rows = jnp.concatenate([jnp.zeros((LANES - T, C_WIDTH), F32), p[:, c0:c0 + C_WIDTH]], axis=0)
            new_t = new_rows.T
            for rb in range(C_WIDTH // LANES):
                rs = slice(rb * LANES, (rb + 1) * LANES)
                rolled = pltpu.roll(c_refs[g][0, 0, part, rs, :], wb - T, axis=1)
                if wb > LANES:
                    kv_refs[g][0, 0, part, rs, 0:wb - LANES] = rolled[:, 0:wb - LANES]
                kv_refs[g][0, 0, part, rs, wb - LANES:wb] = jnp.where(tail, new_t[rs], rolled[:, wb - LANES:wb])


def _attn_sample_call(p, caches, layer_slot, kv_prev):
    B, T, _ = p.shape
    cache_spec = lambda c: pl.BlockSpec((1, 1) + c.shape[2:], lambda b: (layer_slot, b, 0, 0, 0))
    in_specs = [pl.BlockSpec((1, T, ODD_IN_W), lambda b: (b, 0, 0))] + [cache_spec(c) for c in caches]
    args = [p] + list(caches)
    aliases = {}
    n_alias = 0
    if kv_prev is not None:
        n_alias = 3
        in_specs += [pl.BlockSpec(memory_space=pl.ANY)] * 3
        args += list(kv_prev)
        aliases = {4: 1, 5: 2, 6: 3}
    outs = pl.pallas_call(
        functools.partial(_attn_sample_kernel, n_alias),
        grid=(B,),
        in_specs=in_specs,
        out_specs=[pl.BlockSpec((1, T, C_WIDTH), lambda b: (b, 0, 0))] + [cache_spec(c) for c in caches],
        out_shape=[jax.ShapeDtypeStruct((B, T, C_WIDTH), F32)]
        + [jax.ShapeDtypeStruct(c.shape, F32) for c in caches],
        input_output_aliases=aliases,
        compiler_params=_params(("parallel",)),
        name="attn_sample",
    )(*args)
    return outs[0], outs[1:]


def _odd_out_kernel(o_ref, z_ref, x_ref, wout_ref, g_ref, b_ref, out_ref):
    y = _dot(o_ref[...] * _silu(z_ref[...]), wout_ref[...])
    out_ref[...] = _layer_norm(DEEPNORM_ALPHA * x_ref[...] + y, g_ref[...], b_ref[...])


def _odd_out_call(o, p, x, wout, g, b, tm):
    B, T, D = x.shape
    M = B * T
    out = pl.pallas_call(
        _odd_out_kernel,
        grid=(M // tm,),
        in_specs=[
            pl.BlockSpec((tm, C_WIDTH), lambda i: (i, 0)),
            pl.BlockSpec((tm, C_WIDTH), lambda i: (i, C_QKV_W // C_WIDTH)),
            pl.BlockSpec((tm, D), lambda i: (i, 0)),
            _full_spec(wout.shape), _full_spec(g.shape), _full_spec(b.shape),
        ],
        out_specs=pl.BlockSpec((tm, D), lambda i: (i, 0)),
        out_shape=jax.ShapeDtypeStruct((M, D), F32),
        compiler_params=_params(("parallel",)),
        name="odd_out",
    )(o.reshape(M, C_WIDTH), p.reshape(M, ODD_IN_W), x.reshape(M, D), wout, g, b)
    return out.reshape(B, T, D)


def kernel(x_prompt, x_sample, state_rwkv, state_shift, state_conv, cache_kv_w128, cache_kv_w512, cache_kv_w2048, even_w_in, even_w_out, rwkv_mu, rwkv_w0, rwkv_w2, rwkv_a0, rwkv_a2, rwkv_k_k, rwkv_k_a, rwkv_r_k, rwkv_gn_g, rwkv_gn_b, conv_w, odd_w_in, odd_w_out, ln_g, ln_b):
    xp, xs = x_prompt, x_sample
    BP, TP, _ = xp.shape
    BS, TS, _ = xs.shape
    row = lambda t: t.reshape(1, -1)
    to_channel_major = lambda c: jnp.transpose(c, (0, 1, 3, 4, 5, 2)).reshape(
        c.shape[0], c.shape[1], 2, C_WIDTH, c.shape[2])
    caches = [to_channel_major(c) for c in (cache_kv_w128, cache_kv_w512, cache_kv_w2048)]
    rwkv_p, rwkv_s, shift_p, shift_s, conv_p, conv_s = [], [], [], [], [], []
    kv_p = kv_s = None
    for li in range(DEPTH):
        j = li // 2
        g, b = row(ln_g[li]), row(ln_b[li])
        if li % 2 == 0:
            w_in = even_w_in[j].astype(BF16)
            w_a, w_rest = w_in[:, :A_SHIFT_W], w_in[:, A_SHIFT_W:]
            w_out = even_w_out[j].astype(BF16)
            zeros_lora = jnp.zeros((HEAD, A_WIDTH), F32)
            rw = (row(rwkv_mu[j]), row(rwkv_w0[j]), _stack_split(jnp.concatenate([rwkv_w2[j], zeros_lora], 0)),
                  row(rwkv_a0[j]), _stack_split(jnp.concatenate([zeros_lora, rwkv_a2[j]], 0)), row(rwkv_k_k[j]),
                  row(rwkv_k_a[j]), row(rwkv_r_k[j]), row(rwkv_gn_g[j]), row(rwkv_gn_b[j]))
            pa, pb = _proj_call(xp, [w_a, w_rest], tm=512, out_dtypes=[F32, BF16])
            ya, s_new = _wkv_call(pa, jnp.zeros((BP, 1, A_SHIFT_W), F32),
                                  jnp.zeros((BP, 2 * N_PAIRS, HEAD, HEAD), F32), rw,
                                  C=WKV_CHUNK, TB=8 * WKV_CHUNK, CPI=4)
            xp, ulast = _even_out_call(ya, pb, jnp.zeros((BP, 8, B_WIDTH), F32), xp, w_out, conv_w[j], g, b,
                                       TB=4 * WKV_CHUNK)
            rwkv_p.append(s_new)
            shift_p.append(pa[:, -1, :])
            conv_p.append(ulast[:, 6:8, :])
            pa, pb = _proj_call(xs, [w_a, w_rest], tm=BS * TS, out_dtypes=[F32, BF16])
            ya, s_new = _wkv_call(pa, state_shift[j].reshape(BS, 1, A_SHIFT_W), state_rwkv[j], rw,
                                  C=TS, TB=TS, CPI=1, NSEQ=8)
            u0 = jnp.concatenate([jnp.zeros((BS, 6, B_WIDTH), F32), state_conv[j]], axis=1)
            xs, ulast = _even_out_call(ya, pb, u0, xs, w_out, conv_w[j], g, b, TB=TS)
            rwkv_s.append(s_new)
            shift_s.append(pa[:, -1, :])
            conv_s.append(ulast[:, 6:8, :])
        else:
            w_in = odd_w_in[j].astype(BF16)
            w_out = odd_w_out[j].astype(BF16)
            pp, kv_p = _proj_odd_prompt_call(xp, w_in, j, kv_p)
            op = _attn_prompt_call(pp)
            xp = _odd_out_call(op, pp, xp, w_out, g, b, tm=512)
            (ps,) = _proj_call(xs, [w_in], tm=BS * TS)
            os_, kv_s = _attn_sample_call(ps, caches, j, kv_s)
            xs = _odd_out_call(os_, ps, xs, w_out, g, b, tm=BS * TS)
    kv_shape = lambda t: jnp.transpose(
        t.reshape(t.shape[0], t.shape[1], 2, C_WIDTH // HEAD, HEAD, t.shape[4]), (0, 1, 5, 2, 3, 4))
    return (xp, xs,
            jnp.stack(rwkv_p), jnp.stack(rwkv_s),
            jnp.stack(shift_p), jnp.stack(shift_s),
            jnp.stack(conv_p), jnp.stack(conv_s),
            kv_shape(kv_p[0]), kv_shape(kv_s[0]),
            kv_shape(kv_p[1]), kv_shape(kv_s[1]),
            kv_shape(kv_p[2]), kv_shape(kv_s[2]))
```
